```python
import math
import jax, jax.numpy as jnp
from jax import lax
import numpy as np

D_MODEL = 1024
BATCH = 4
SEQ = 4096
DEPTH = 4

MIX_WIDTH = D_MODEL
ATTN_HEAD_DIM = 64
ATTN_WIDTH = D_MODEL // 2
N_ATTN_HEADS = ATTN_WIDTH // ATTN_HEAD_DIM
DILATED_PATTERNS = ((128, 1), (512, 4), (2048, 16))
N_MLSTM_HEADS = 4
MLSTM_WIDTH = MIX_WIDTH - ATTN_WIDTH
MLSTM_HEAD_DIM = MLSTM_WIDTH // N_MLSTM_HEADS
MLSTM_CHUNK = 128
MLSTM_CONV = 4
IN_COLS = 3 * ATTN_WIDTH + 4 * MLSTM_WIDTH + 2 * N_MLSTM_HEADS
N_BUCKETS = 32
MAX_DISTANCE = 2048
N_MEM = 256
N_XHEADS = 4
XHEAD_DIM = D_MODEL // N_XHEADS
D_FF = 2816
FFN_CONV = 3
EPS = 1e-6

kernel_name = "hybrid_dilattn_mlstm_memxattn_convffn"


def rmsnorm(x, g):
    x32 = x.astype(jnp.float32)
    y = x32 * lax.rsqrt(jnp.mean(x32 * x32, axis=-1, keepdims=True) + EPS)
    return (y * g.astype(jnp.float32)).astype(x.dtype)


def causal_dwconv(x, w, b):
    K = w.shape[0]
    S = x.shape[1]
    xp = jnp.pad(x, ((0, 0), (K - 1, 0), (0, 0)))
    y = b + xp[:, 0:S] * w[0]
    for j in range(1, K):
        y = y + xp[:, j:j + S] * w[j]
    return y


def t5_causal_bucket(dist):
    max_exact = N_BUCKETS // 2
    d_f = jnp.maximum(dist, 1).astype(jnp.float32)
    large = max_exact + (jnp.log(d_f / max_exact) / math.log(MAX_DISTANCE / max_exact)
                         * (N_BUCKETS - max_exact)).astype(jnp.int32)
    large = jnp.minimum(large, N_BUCKETS - 1)
    return jnp.where(dist < max_exact, dist, large)


def dilated_window_attention(q, k, v, bias_sub, dil):
    B, H, S, Dh = q.shape
    n = bias_sub.shape[1] - 1
    blk = n
    span = dil * blk
    L = -(-S // span) * span
    M = L // dil
    nb = M // blk

    def to_blocks(t):
        t = jnp.pad(t, ((0, 0), (0, 0), (0, L - S), (0, 0)))
        t = t.reshape(B, H, M, dil, Dh).swapaxes(2, 3)
        return t.reshape(B, H, dil, nb, blk, Dh)

    def with_prev(t):
        prev = jnp.pad(t, ((0, 0), (0, 0), (0, 0), (1, 0), (0, 0), (0, 0)))[:, :, :, :nb]
        return jnp.concatenate([prev, t], axis=-2)

    qb = to_blocks(q)
    kw = with_prev(to_blocks(k))
    vw = with_prev(to_blocks(v))
    logits = jnp.einsum("bhcnqd,bhcnkd->bhcnqk", qb, kw) * (Dh ** -0.5)
    qi = jnp.arange(blk)[:, None]
    kj = jnp.arange(2 * blk)[None, :]
    dist = qi + blk - kj
    band = (dist >= 0) & (dist <= n)
    bias = bias_sub[:, jnp.clip(dist, 0, n)]
    has_prev = (jnp.arange(nb)[:, None, None] > 0) | (kj[None] >= blk)
    valid = band[None] & has_prev
    logits = jnp.where(valid, logits + bias[:, None, None], -jnp.inf)
    m = jnp.max(logits, axis=-1, keepdims=True)
    p = jnp.exp(logits - m)
    s = jnp.sum(p, axis=-1, keepdims=True)
    o = jnp.einsum("bhcnqk,bhcnkd->bhcnqd", p, vw) / s
    lse = (m + jnp.log(s))[..., 0]
    o = o.reshape(B, H, dil, M, Dh).swapaxes(2, 3).reshape(B, H, L, Dh)[:, :, :S]
    lse = lse.reshape(B, H, dil, M).swapaxes(2, 3).reshape(B, H, L)[:, :, :S]
    return o, lse


def mlstm_chunkwise(q, k, v, i_pre, f_pre):
    B, H, S, Dh = q.shape
    nc = S // MLSTM_CHUNK

    def cs(t):
        return t.reshape((B, H, nc, MLSTM_CHUNK) + t.shape[3:])

    qc, kc, vc = cs(q), cs(k), cs(v)
    ig = cs(i_pre)
    b = jnp.cumsum(cs(jax.nn.log_sigmoid(f_pre)), axis=-1)
    g = b[..., -1]
    w = g[..., None] - b + ig

    def step(carry, inp):
        C, n, m = carry
        k_c, v_c, w_c, g_c = inp
        m_new = jnp.maximum(g_c + m, jnp.max(w_c, axis=-1))
        decay = jnp.exp(g_c + m - m_new)
        wt = jnp.exp(w_c - m_new[..., None])
        C_new = decay[..., None, None] * C + jnp.einsum("bhs,bhsv,bhsk->bhvk", wt, v_c, k_c)
        n_new = decay[..., None] * n + jnp.einsum("bhs,bhsk->bhk", wt, k_c)
        return (C_new, n_new, m_new), (C, n, m)

    init = (jnp.zeros((B, H, Dh, Dh), jnp.float32),
            jnp.zeros((B, H, Dh), jnp.float32),
            jnp.zeros((B, H), jnp.float32))
    lead = lambda t: jnp.moveaxis(t, 2, 0)
    _, (C_prev, n_prev, m_prev) = lax.scan(step, init, (lead(kc), lead(vc), lead(w), lead(g)))
    C_prev = jnp.moveaxis(C_prev, 0, 2)
    n_prev = jnp.moveaxis(n_prev, 0, 2)
    m_prev = jnp.moveaxis(m_prev, 0, 2)

    t_idx = jnp.arange(MLSTM_CHUNK)
    causal = t_idx[:, None] >= t_idx[None, :]
    D = jnp.where(causal, b[..., :, None] - b[..., None, :] + ig[..., None, :], -jnp.inf)
    a = b + m_prev[..., None]
    m_t = jnp.maximum(a, jnp.max(D, axis=-1))
    P = jnp.exp(D - m_t[..., None]) * jnp.einsum("bhntd,bhnsd->bhnts", qc, kc)
    inter = jnp.exp(a - m_t)
    num = (inter[..., None] * jnp.einsum("bhnvk,bhntk->bhntv", C_prev, qc)
           + jnp.einsum("bhnts,bhnsv->bhntv", P, vc))
    den = inter * jnp.einsum("bhnk,bhntk->bhnt", n_prev, qc) + jnp.sum(P, axis=-1)
    h = num / jnp.maximum(jnp.abs(den), jnp.exp(-m_t))[..., None]
    return h.reshape(B, H, S, Dh)


def hybrid_mixer(h, rel_bias, w_in, mconv_w, mconv_b, b_ig, b_fg, g_attn_out, g_mlstm_out, w_out):
    B, S, _ = h.shape
    f32 = jnp.float32
    proj = h @ w_in
    sizes = [ATTN_WIDTH] * 3 + [MLSTM_WIDTH] * 4 + [N_MLSTM_HEADS] * 2
    idx = [int(s) for s in np.cumsum(sizes)[:-1]]
    qa, ka, va, qm, km, vm, om, ig, fg = jnp.split(proj, idx, axis=-1)

    def heads(t, nh):
        return t.reshape(B, S, nh, -1).transpose(0, 2, 1, 3)

    qa_ = heads(qa, N_ATTN_HEADS).astype(f32)
    ka_ = heads(ka, N_ATTN_HEADS).astype(f32)
    va_ = heads(va, N_ATTN_HEADS).astype(f32)
    outs, lses = [], []
    for window, dil in DILATED_PATTERNS:
        n = window // dil
        bias_sub = rel_bias[t5_causal_bucket(jnp.arange(n + 1) * dil)].T.astype(f32)
        o, l = dilated_window_attention(qa_, ka_, va_, bias_sub, dil)
        outs.append(o)
        lses.append(l)
    mix_w = jax.nn.softmax(jnp.stack(lses), axis=0)
    attn = jnp.einsum("pbhs,pbhsd->bhsd", mix_w, jnp.stack(outs))
    attn = attn.transpose(0, 2, 1, 3).reshape(B, S, ATTN_WIDTH).astype(h.dtype)

    qk = jax.nn.silu(causal_dwconv(jnp.concatenate([qm, km], axis=-1), mconv_w, mconv_b))
    qm, km = jnp.split(qk, 2, axis=-1)
    q = heads(qm, N_MLSTM_HEADS).astype(f32)
    k = heads(km, N_MLSTM_HEADS).astype(f32) * (MLSTM_HEAD_DIM ** -0.5)
    v = heads(vm, N_MLSTM_HEADS).astype(f32)
    i_pre = (ig + b_ig).astype(f32).transpose(0, 2, 1)
    f_pre = (fg + b_fg).astype(f32).transpose(0, 2, 1)
    hm = mlstm_chunkwise(q, k, v, i_pre, f_pre)
    hm = jax.nn.sigmoid(om.astype(f32)) * hm.transpose(0, 2, 1, 3).reshape(B, S, MLSTM_WIDTH)

    mixed = jnp.concatenate([rmsnorm(attn, g_attn_out), rmsnorm(hm.astype(h.dtype), g_mlstm_out)], axis=-1)
    return mixed @ w_out


def memory_cross_attention(h, mem, wq, wk, wv, wo):
    B, S, D = h.shape
    f32 = jnp.float32
    q = (h @ wq).reshape(B, S, N_XHEADS, XHEAD_DIM).astype(f32)
    k = (mem @ wk).reshape(B, -1, N_XHEADS, XHEAD_DIM).astype(f32)
    v = (mem @ wv).reshape(B, -1, N_XHEADS, XHEAD_DIM).astype(f32)
    p = jax.nn.softmax(jnp.einsum("bshd,bmhd->bhsm", q, k) * (XHEAD_DIM ** -0.5), axis=-1)
    o = jnp.einsum("bhsm,bmhd->bshd", p, v).reshape(B, S, D).astype(h.dtype)
    return o @ wo


def conv_ffn(h, w_up, conv_w, conv_b, w_down):
    u = causal_dwconv(h @ w_up, conv_w, conv_b)
    a, g = jnp.split(u, 2, axis=-1)
    return (jax.nn.gelu(g, approximate=True) * a) @ w_down


def setup_inputs(seed: int = 0) -> dict:
    key = jax.random.key(seed)
    ks = jax.random.split(key, 32)
    f32 = jnp.float32

    def nrm(k, shape, scale):
        return scale * jax.random.normal(k, shape, f32)

    def gain(k, shape):
        return 1.0 + 0.02 * jax.random.normal(k, shape, f32)

    L, D = DEPTH, D_MODEL
    return {
        "x": nrm(ks[0], (BATCH, SEQ, D), 1.0),
        "mem": nrm(ks[1], (BATCH, N_MEM, D), 1.0),
        "rel_bias": nrm(ks[2], (N_BUCKETS, N_ATTN_HEADS), 0.2),
        "pre_mix_g": gain(ks[3], (L, D)),
        "w_in": nrm(ks[4], (L, D, IN_COLS), D ** -0.5),
        "mconv_w": nrm(ks[5], (L, MLSTM_CONV, 2 * MLSTM_WIDTH), MLSTM_CONV ** -0.5),
        "mconv_b": nrm(ks[6], (L, 2 * MLSTM_WIDTH), 0.02),
        "b_igate": nrm(ks[7], (L, N_MLSTM_HEADS), 0.1),
        "b_fgate": jnp.linspace(3.0, 6.0, N_MLSTM_HEADS, dtype=f32)[None] + nrm(ks[8], (L, N_MLSTM_HEADS), 0.1),
        "attn_out_g": gain(ks[9], (L, ATTN_WIDTH)),
        "mlstm_out_g": gain(ks[10], (L, MLSTM_WIDTH)),
        "w_out": nrm(ks[11], (L, MIX_WIDTH, D), MIX_WIDTH ** -0.5),
        "post_mix_g": gain(ks[12], (L, D)),
        "pre_mem_g": gain(ks[13], (L, D)),
        "wq_mem": nrm(ks[14], (L, D, D), D ** -0.5),
        "wk_mem": nrm(ks[15], (L, D, D), D ** -0.5),
        "wv_mem": nrm(ks[16], (L, D, D), D ** -0.5),
        "wo_mem": nrm(ks[17], (L, D, D), D ** -0.5),
        "post_mem_g": gain(ks[18], (L, D)),
        "pre_ffn_g": gain(ks[19], (L, D)),
        "w_up": nrm(ks[20], (L, D, 2 * D_FF), D ** -0.5),
        "fconv_w": nrm(ks[21], (L, FFN_CONV, 2 * D_FF), FFN_CONV ** -0.5),
        "fconv_b": nrm(ks[22], (L, 2 * D_FF), 0.02),
        "w_down": nrm(ks[23], (L, D_FF, D), D_FF ** -0.5),
        "post_ffn_g": gain(ks[24], (L, D)),
    }


def reference(x, mem, rel_bias, pre_mix_g, w_in, mconv_w, mconv_b, b_igate, b_fgate,
              attn_out_g, mlstm_out_g, w_out, post_mix_g, pre_mem_g, wq_mem, wk_mem,
              wv_mem, wo_mem, post_mem_g, pre_ffn_g, w_up, fconv_w, fconv_b, w_down,
              post_ffn_g):
    for l in range(DEPTH):
        h = hybrid_mixer(rmsnorm(x, pre_mix_g[l]), rel_bias, w_in[l], mconv_w[l], mconv_b[l],
                         b_igate[l], b_fgate[l], attn_out_g[l], mlstm_out_g[l], w_out[l])
        x = x + rmsnorm(h, post_mix_g[l])
        h = memory_cross_attention(rmsnorm(x, pre_mem_g[l]), mem, wq_mem[l], wk_mem[l],
                                   wv_mem[l], wo_mem[l])
        x = x + rmsnorm(h, post_mem_g[l])
        h = conv_ffn(rmsnorm(x, pre_ffn_g[l]), w_up[l], fconv_w[l], fconv_b[l], w_down[l])
        x = x + rmsnorm(h, post_ffn_g[l])
    return x
```

```python
import functools
import math

import jax
import jax.numpy as jnp
import numpy as np
from jax import lax
from jax.experimental import pallas as pl
from jax.experimental.pallas import tpu as pltpu

F32 = jnp.float32
BF16 = jnp.bfloat16

D_MODEL = 1024
BATCH = 4
SEQ = 4096
DEPTH = 4
TOKENS = BATCH * SEQ

ATTN_HEAD_DIM = 64
ATTN_WIDTH = 512
N_ATTN_HEADS = 8
DILATIONS = (1, 4, 16)
ATTN_BLK = 128
N_MLSTM_HEADS = 4
MLSTM_WIDTH = 512
MLSTM_HEAD_DIM = 128
MLSTM_CHUNK = 128
MLSTM_CONV = 4
N_BUCKETS = 32
MAX_DISTANCE = 2048
N_MEM = 256
N_XHEADS = 4
XHEAD_DIM = 256
D_FF = 2816
FFN_CONV = 3
EPS = 1e-6

LANES = 128
SUBLANES = 8
NEG = -1e30
VMEM_LIMIT = 56 * 1024 * 1024

TM_IN = 512
TM_OUT = 512
TM_FFN = 256
FFN_COLS = 256
MAIN_COLS = 3 * ATTN_WIDTH + 4 * MLSTM_WIDTH

NT = (((1,), (1,)), ((), ()))
TN = (((0,), (0,)), ((), ()))


def _dot(a, b):
    return jnp.dot(a, b, preferred_element_type=F32)


def _dot_nt(a, b):
    return lax.dot_general(a, b, NT, preferred_element_type=F32)


def _rms(x, g):
    return x * lax.rsqrt(jnp.mean(x * x, axis=-1, keepdims=True) + EPS) * g


def _sigmoid(x):
    return 1.0 / (1.0 + jnp.exp(-x))


def _resident(block_shape, index_map):
    return pl.BlockSpec(block_shape, index_map, pipeline_mode=pl.Buffered(1))


def _params(*semantics):
    return pltpu.CompilerParams(dimension_semantics=semantics, vmem_limit_bytes=VMEM_LIMIT)


def _memkv_kernel(mem_ref, w_ref, out_ref):
    out_ref[...] = _dot(mem_ref[...], w_ref[...]).astype(BF16)


def _memkv(mem_bf, wkv_bf):
    rows = BATCH * N_MEM
    return pl.pallas_call(
        _memkv_kernel,
        grid=(DEPTH, 2),
        in_specs=[
            _resident((rows, D_MODEL), lambda l, j: (0, 0)),
            pl.BlockSpec((None, None, D_MODEL, D_MODEL), lambda l, j: (l, j, 0, 0)),
        ],
        out_specs=pl.BlockSpec((None, None, rows, D_MODEL), lambda l, j: (l, j, 0, 0)),
        out_shape=jax.ShapeDtypeStruct((DEPTH, 2, rows, D_MODEL), BF16),
        compiler_params=_params("parallel", "parallel"),
        name="memkv",
    )(mem_bf, wkv_bf)


def _mix_in_kernel(x_ref, g_ref, w_ref, wg_ref, cw_ref, cb_ref, ksc_ref,
                   qkv_ref, qk_ref, v_ref, om_ref, gate_ref, buf_ref):
    @pl.when(pl.program_id(0) % (SEQ // TM_IN) == 0)
    def _():
        buf_ref[0:SUBLANES, :] = jnp.zeros((SUBLANES, 2 * MLSTM_WIDTH), F32)

    h = _rms(x_ref[...], g_ref[...]).astype(BF16)
    a0 = 3 * ATTN_WIDTH
    a1 = a0 + 2 * MLSTM_WIDTH
    a2 = a1 + MLSTM_WIDTH
    qkv_ref[...] = _dot(h, w_ref[:, 0:a0]).astype(BF16)

    pre = _dot(h, w_ref[:, a0:a1])
    buf_ref[SUBLANES:SUBLANES + TM_IN, :] = pre
    y = cb_ref[...]
    for j in range(MLSTM_CONV - 1):
        lo = SUBLANES - (MLSTM_CONV - 1) + j
        y = y + buf_ref[lo:lo + TM_IN, :] * cw_ref[j:j + 1, :]
    y = y + pre * cw_ref[MLSTM_CONV - 1:MLSTM_CONV, :]
    buf_ref[0:SUBLANES, :] = buf_ref[TM_IN:TM_IN + SUBLANES, :]
    qk_ref[...] = (y * _sigmoid(y) * ksc_ref[...]).astype(BF16)

    v_ref[...] = _dot(h, w_ref[:, a1:a2]).astype(BF16)
    om_ref[...] = _dot(h, w_ref[:, a2:MAIN_COLS]).astype(BF16)
    gate_ref[...] = _dot(h, wg_ref[...])


def _mix_in(l, x, pre_g, w_main, w_gate, cw, cb, ksc):
    row = lambda i: (i, 0)
    lay = lambda i: (l, 0, 0)
    return pl.pallas_call(
        _mix_in_kernel,
        grid=(TOKENS // TM_IN,),
        in_specs=[
            pl.BlockSpec((TM_IN, D_MODEL), row),
            _resident((None, 1, D_MODEL), lay),
            _resident((None, D_MODEL, MAIN_COLS), lay),
            _resident((None, D_MODEL, LANES), lay),
            _resident((None, MLSTM_CONV, 2 * MLSTM_WIDTH), lay),
            _resident((None, 1, 2 * MLSTM_WIDTH), lay),
            _resident((1, 2 * MLSTM_WIDTH), lambda i: (0, 0)),
        ],
        out_specs=[
            pl.BlockSpec((TM_IN, 3 * ATTN_WIDTH), row),
            pl.BlockSpec((TM_IN, 2 * MLSTM_WIDTH), row),
            pl.BlockSpec((TM_IN, MLSTM_WIDTH), row),
            pl.BlockSpec((TM_IN, MLSTM_WIDTH), row),
            pl.BlockSpec((TM_IN, LANES), row),
        ],
        out_shape=[
            jax.ShapeDtypeStruct((TOKENS, 3 * ATTN_WIDTH), BF16),
            jax.ShapeDtypeStruct((TOKENS, 2 * MLSTM_WIDTH), BF16),
            jax.ShapeDtypeStruct((TOKENS, MLSTM_WIDTH), BF16),
            jax.ShapeDtypeStruct((TOKENS, MLSTM_WIDTH), BF16),
            jax.ShapeDtypeStruct((TOKENS, LANES), F32),
        ],
        scratch_shapes=[pltpu.VMEM((TM_IN + SUBLANES, 2 * MLSTM_WIDTH), F32)],
        compiler_params=_params("arbitrary"),
        name="mix_in",
    )(x, pre_g, w_main, w_gate, cw, cb, ksc)


N_UNITS = SEQ // ATTN_BLK


def _dil_attn_kernel(q_ref, k_ref, v_ref, bias_ref, out_ref,
                     qf, kf, vf, dq, dk, dv, o_s0, o_s1, o_s2, l_s0, l_s1, l_s2):
    conv_rows = 256

    def to_f32(r, carry):
        rows = pl.ds(pl.multiple_of(r * conv_rows, conv_rows), conv_rows)
        qf[rows, :] = q_ref[rows, :].astype(F32) * (ATTN_HEAD_DIM ** -0.5)
        kf[rows, :] = k_ref[rows, :].astype(F32)
        vf[rows, :] = v_ref[rows, :].astype(F32)
        return carry

    lax.fori_loop(0, SEQ // conv_rows, to_f32, 0)

    lane = lax.broadcasted_iota(jnp.int32, (ATTN_BLK, LANES), 1)
    first_head = lane < ATTN_HEAD_DIM
    in_prev_half = lax.broadcasted_iota(jnp.int32, (ATTN_BLK, 2 * ATTN_BLK), 1) < ATTN_BLK

    def run_pattern(p, d, o_scr, l_scr):
        nb = N_UNITS // d
        shift = int(math.log2(nb))

        def start_of(u):
            c = lax.shift_right_logical(u, shift)
            n = lax.bitwise_and(u, nb - 1)
            return c + n * (ATTN_BLK * d), n

        def deinterleave(u, carry):
            start, _ = start_of(u)
            src = pl.ds(start, ATTN_BLK, stride=d)
            dst = pl.ds(pl.multiple_of(u * ATTN_BLK, ATTN_BLK), ATTN_BLK)
            dq[dst, :] = qf[src, :].astype(BF16)
            dk[dst, :] = kf[src, :].astype(BF16)
            dv[dst, :] = vf[src, :].astype(BF16)
            return carry

        lax.fori_loop(0, N_UNITS, deinterleave, 0)

        def unit(u, carry):
            start, n = start_of(u)
            cur = pl.ds(pl.multiple_of(u * ATTN_BLK, ATTN_BLK), ATTN_BLK)
            prev = pl.ds(pl.multiple_of(jnp.maximum(u - 1, 0) * ATTN_BLK, ATTN_BLK), ATTN_BLK)
            q = dq[cur, :]
            kwin = jnp.concatenate([dk[prev, :], dk[cur, :]], axis=0)
            vwin = jnp.concatenate([dv[prev, :], dv[cur, :]], axis=0)
            no_prev = jnp.where(in_prev_half, jnp.where(n > 0, 0.0, NEG), 0.0)
            o_pair, lse_pair = [], []
            for hh in range(2):
                sel = first_head if hh == 0 else jnp.logical_not(first_head)
                qh = jnp.where(sel, q, jnp.zeros_like(q))
                logits = _dot_nt(qh, kwin) + bias_ref[p, hh] + no_prev
                m = jnp.max(logits, axis=1, keepdims=True)
                e = jnp.exp(logits - m)
                s = jnp.sum(e, axis=1, keepdims=True)
                o_pair.append(_dot(e.astype(BF16), vwin) / s)
                lse_pair.append(m + jnp.log(s))
            dst = pl.ds(start, ATTN_BLK, stride=d)
            o_scr[dst, :] = jnp.where(first_head, o_pair[0], o_pair[1])
            l_scr[dst, :] = jnp.where(first_head, lse_pair[0], lse_pair[1])
            return carry

        lax.fori_loop(0, N_UNITS, unit, 0)

    run_pattern(0, DILATIONS[0], o_s0, l_s0)
    run_pattern(1, DILATIONS[1], o_s1, l_s1)
    run_pattern(2, DILATIONS[2], o_s2, l_s2)

    def combine(r, carry):
        rows = pl.ds(pl.multiple_of(r * ATTN_BLK, ATTN_BLK), ATTN_BLK)
        la, lb, lc = l_s0[rows, :], l_s1[rows, :], l_s2[rows, :]
        mx = jnp.maximum(jnp.maximum(la, lb), lc)
        ea, eb, ec = jnp.exp(la - mx), jnp.exp(lb - mx), jnp.exp(lc - mx)
        mixed = (ea * o_s0[rows, :] + eb * o_s1[rows, :] + ec * o_s2[rows, :]) / (ea + eb + ec)
        out_ref[rows, :] = mixed.astype(BF16)
        return carry

    lax.fori_loop(0, N_UNITS, combine, 0)


def _dil_attn(qkv, bias):
    pairs = N_ATTN_HEADS // 2
    blk = (SEQ, LANES)
    f32_seq = pltpu.VMEM((SEQ, LANES), F32)
    bf_seq = pltpu.VMEM((SEQ, LANES), BF16)
    return pl.pallas_call(
        _dil_attn_kernel,
        grid=(BATCH, pairs),
        in_specs=[
            pl.BlockSpec(blk, lambda b, hp: (b, hp)),
            pl.BlockSpec(blk, lambda b, hp: (b, pairs + hp)),
            pl.BlockSpec(blk, lambda b, hp: (b, 2 * pairs + hp)),
            pl.BlockSpec((len(DILATIONS), 2, ATTN_BLK, 2 * ATTN_BLK), lambda b, hp: (0, hp, 0, 0)),
        ],
        out_specs=pl.BlockSpec(blk, lambda b, hp: (b, hp)),
        out_shape=jax.ShapeDtypeStruct((TOKENS, ATTN_WIDTH), BF16),
        scratch_shapes=[f32_seq] * 3 + [bf_seq] * 3 + [f32_seq] * 6,
        compiler_params=_params("parallel", "parallel"),
        name="dil_attn",
    )(qkv, qkv, qkv, bias)


def _t5_causal_bucket(dist):
    max_exact = N_BUCKETS // 2
    d_f = jnp.maximum(dist, 1).astype(F32)
    large = max_exact + (jnp.log(d_f / max_exact) / math.log(MAX_DISTANCE / max_exact)
                         * (N_BUCKETS - max_exact)).astype(jnp.int32)
    large = jnp.minimum(large, N_BUCKETS - 1)
    return jnp.where(dist < max_exact, dist, large)


def _attn_bias_table(rel_bias):
    qi = np.arange(ATTN_BLK)[:, None]
    kj = np.arange(2 * ATTN_BLK)[None, :]
    dist = qi + ATTN_BLK - kj
    band = jnp.asarray((dist >= 0) & (dist <= ATTN_BLK))
    idx = np.clip(dist, 0, ATTN_BLK)
    tables = []
    for d in DILATIONS:
        bias_sub = rel_bias[_t5_causal_bucket(jnp.arange(ATTN_BLK + 1) * d)].T.astype(F32)
        tables.append(jnp.where(band[None], bias_sub[:, idx], NEG))
    return jnp.stack(tables)


def _mlstm_kernel(qk_ref, v_ref, om_ref, gate_ref, gbias_ref, gout_ref, out_ref,
                  c_scr, n_scr, m_scr):
    @pl.when(pl.program_id(1) == 0)
    def _():
        c_scr[...] = jnp.zeros_like(c_scr)
        n_scr[...] = jnp.zeros_like(n_scr)
        m_scr[...] = jnp.zeros_like(m_scr)

    nh = N_MLSTM_HEADS
    ch = MLSTM_CHUNK
    row = lax.broadcasted_iota(jnp.int32, (ch, LANES), 0)
    lane = lax.broadcasted_iota(jnp.int32, (ch, LANES), 1)
    causal = row >= lane

    gates = gate_ref[...] + gbias_ref[...]
    logsig = jnp.minimum(gates, 0.0) - jnp.log1p(jnp.exp(-jnp.abs(gates)))
    logf = jnp.where((lane >= nh) & (lane < 2 * nh), logsig, 0.0)
    tri = causal.astype(BF16)
    hi = logf.astype(BF16)
    r1 = logf - hi.astype(F32)
    mid = r1.astype(BF16)
    lo = (r1 - mid.astype(F32)).astype(BF16)
    bcum = _dot(tri, hi) + _dot(tri, mid) + _dot(tri, lo)
    rows_form = jnp.where(lane < nh, gates, bcum).T

    outs = []
    for h in range(nh):
        hs = slice(h * MLSTM_HEAD_DIM, (h + 1) * MLSTM_HEAD_DIM)
        ks = slice(MLSTM_WIDTH + h * MLSTM_HEAD_DIM, MLSTM_WIDTH + (h + 1) * MLSTM_HEAD_DIM)
        b_col = bcum[:, nh + h:nh + h + 1]
        i_col = gates[:, h:h + 1]
        b_row = rows_form[nh + h:nh + h + 1, :]
        i_row = rows_form[h:h + 1, :]
        g = bcum[ch - 1:ch, nh + h:nh + h + 1]
        m_prev = m_scr[h:h + 1, 0:1]
        q_h = qk_ref[:, hs]
        k_h = qk_ref[:, ks]
        v_h = v_ref[:, hs]

        dmat = jnp.where(causal, b_col - b_row + i_row, NEG)
        a = b_col + m_prev
        m_t = jnp.maximum(a, jnp.max(dmat, axis=1, keepdims=True))
        p = jnp.exp(dmat - m_t) * _dot_nt(q_h, k_h)
        inter = jnp.exp(a - m_t)
        c_prev = c_scr[h]
        n_prev = n_scr[h:h + 1, :]
        num = inter * _dot_nt(q_h, c_prev.astype(BF16)) + _dot(p.astype(BF16), v_h)
        den = (inter * jnp.sum(q_h.astype(F32) * n_prev, axis=1, keepdims=True)
               + jnp.sum(p, axis=1, keepdims=True))
        hv = num / jnp.maximum(jnp.abs(den), jnp.exp(-m_t))
        outs.append(_sigmoid(om_ref[:, hs].astype(F32)) * hv)

        w_col = g - b_col + i_col
        m_new = jnp.maximum(g + m_prev, jnp.max(w_col, axis=0, keepdims=True))
        decay = jnp.exp(g + m_prev - m_new)
        wt = jnp.exp(w_col - m_new)
        vw = (v_h.astype(F32) * wt).astype(BF16)
        c_scr[h] = decay * c_prev + lax.dot_general(vw, k_h, TN, preferred_element_type=F32)
        n_scr[h:h + 1, :] = decay * n_prev + jnp.sum(k_h.astype(F32) * wt, axis=0, keepdims=True)
        m_scr[h:h + 1, :] = jnp.broadcast_to(m_new, (1, LANES))

    hm = jnp.concatenate(outs, axis=1)
    out_ref[...] = _rms(hm, gout_ref[...]).astype(BF16)


def _mlstm(l, qk, v, om, gates, gbias, gout):
    nc = SEQ // MLSTM_CHUNK
    row = lambda b, c: (b * nc + c, 0)
    lay = lambda b, c: (l, 0, 0)
    return pl.pallas_call(
        _mlstm_kernel,
        grid=(BATCH, nc),
        in_specs=[
            pl.BlockSpec((MLSTM_CHUNK, 2 * MLSTM_WIDTH), row),
            pl.BlockSpec((MLSTM_CHUNK, MLSTM_WIDTH), row),
            pl.BlockSpec((MLSTM_CHUNK, MLSTM_WIDTH), row),
            pl.BlockSpec((MLSTM_CHUNK, LANES), row),
            _resident((None, 1, LANES), lay),
            _resident((None, 1, MLSTM_WIDTH), lay),
        ],
        out_specs=pl.BlockSpec((MLSTM_CHUNK, MLSTM_WIDTH), row),
        out_shape=jax.ShapeDtypeStruct((TOKENS, MLSTM_WIDTH), BF16),
        scratch_shapes=[
            pltpu.VMEM((N_MLSTM_HEADS, MLSTM_HEAD_DIM, MLSTM_HEAD_DIM), F32),
            pltpu.VMEM((SUBLANES, MLSTM_HEAD_DIM), F32),
            pltpu.VMEM((SUBLANES, LANES), F32),
        ],
        compiler_params=_params("arbitrary", "arbitrary"),
        name="mlstm",
    )(qk, v, om, gates, gbias, gout)


def _mix_out_kernel(x_ref, attn_ref, hmn_ref, gattn_ref, wout_ref, gpost_ref,
                    gpre_ref, wq_ref, k_ref, v_ref, wo_ref, gpostm_ref, out_ref):
    an = _rms(attn_ref[...].astype(F32), gattn_ref[...]).astype(BF16)
    h = _dot(an, wout_ref[0:ATTN_WIDTH, :]) + _dot(hmn_ref[...], wout_ref[ATTN_WIDTH:, :])
    x1 = x_ref[...] + _rms(h, gpost_ref[...])

    hq = _rms(x1, gpre_ref[...]).astype(BF16)
    q = (_dot(hq, wq_ref[...]) * (XHEAD_DIM ** -0.5)).astype(BF16)
    heads = []
    for hd in range(N_XHEADS):
        sl = slice(hd * XHEAD_DIM, (hd + 1) * XHEAD_DIM)
        logits = _dot_nt(q[:, sl], k_ref[:, sl])
        m = jnp.max(logits, axis=1, keepdims=True)
        e = jnp.exp(logits - m)
        s = jnp.sum(e, axis=1, keepdims=True)
        heads.append((_dot(e.astype(BF16), v_ref[:, sl]) / s).astype(BF16))
    o = jnp.concatenate(heads, axis=1)
    out_ref[...] = x1 + _rms(_dot(o, wo_ref[...]), gpostm_ref[...])


def _mix_out(l, x, attn, hmn, gattn, wout, gpost, gpre, wq, kv, wo, gpostm):
    row = lambda i: (i, 0)
    lay = lambda i: (l, 0, 0)
    tiles_per_seq = SEQ // TM_OUT
    vec = _resident((None, 1, D_MODEL), lay)
    mat = _resident((None, D_MODEL, D_MODEL), lay)
    return pl.pallas_call(
        _mix_out_kernel,
        grid=(TOKENS // TM_OUT,),
        in_specs=[
            pl.BlockSpec((TM_OUT, D_MODEL), row),
            pl.BlockSpec((TM_OUT, ATTN_WIDTH), row),
            pl.BlockSpec((TM_OUT, MLSTM_WIDTH), row),
            _resident((None, 1, ATTN_WIDTH), lay),
            mat, vec, vec, mat,
            pl.BlockSpec((None, None, N_MEM, D_MODEL), lambda i: (l, 0, i // tiles_per_seq, 0)),
            pl.BlockSpec((None, None, N_MEM, D_MODEL), lambda i: (l, 1, i // tiles_per_seq, 0)),
            mat, vec,
        ],
        out_specs=pl.BlockSpec((TM_OUT, D_MODEL), row),
        out_shape=jax.ShapeDtypeStruct((TOKENS, D_MODEL), F32),
        compiler_params=_params("parallel"),
        name="mix_out",
    )(x, attn, hmn, gattn, wout, gpost, gpre, wq, kv, kv, wo, gpostm)


def _ffn_kernel(x_ref, gpre_ref, wup_ref, cw_ref, cb_ref, wdown_ref, gpost_ref, out_ref,
                buf_ref, act_ref):
    @pl.when(pl.program_id(0) % (SEQ // TM_FFN) == 0)
    def _():
        buf_ref[0:SUBLANES, :] = jnp.zeros((SUBLANES, 2 * D_FF), F32)

    x = x_ref[...]
    h = _rms(x, gpre_ref[...]).astype(BF16)

    def conv_cols(c0):
        cols = slice(c0, c0 + FFN_COLS)
        u = _dot(h, wup_ref[:, cols])
        buf_ref[SUBLANES:SUBLANES + TM_FFN, cols] = u
        y = cb_ref[:, cols]
        for j in range(FFN_CONV - 1):
            lo = SUBLANES - (FFN_CONV - 1) + j
            y = y + buf_ref[lo:lo + TM_FFN, cols] * cw_ref[j:j + 1, cols]
        return y + u * cw_ref[FFN_CONV - 1:FFN_CONV, cols]

    for c in range(D_FF // FFN_COLS):
        a = conv_cols(c * FFN_COLS)
        g = conv_cols(D_FF + c * FFN_COLS)
        gelu = 0.5 * g * (1.0 + jnp.tanh(math.sqrt(2.0 / math.pi) * (g + 0.044715 * (g * g * g))))
        act_ref[:, c * FFN_COLS:(c + 1) * FFN_COLS] = (gelu * a).astype(BF16)
    buf_ref[0:SUBLANES, :] = buf_ref[TM_FFN:TM_FFN + SUBLANES, :]

    out_ref[...] = x + _rms(_dot(act_ref[...], wdown_ref[...]), gpost_ref[...])


def _ffn(l, x, gpre, wup, cw, cb, wdown, gpost):
    row = lambda i: (i, 0)
    lay = lambda i: (l, 0, 0)
    vec = _resident((None, 1, D_MODEL), lay)
    return pl.pallas_call(
        _ffn_kernel,
        grid=(TOKENS // TM_FFN,),
        in_specs=[
            pl.BlockSpec((TM_FFN, D_MODEL), row),
            vec,
            _resident((None, D_MODEL, 2 * D_FF), lay),
            _resident((None, FFN_CONV, 2 * D_FF), lay),
            _resident((None, 1, 2 * D_FF), lay),
            _resident((None, D_FF, D_MODEL), lay),
            vec,
        ],
        out_specs=pl.BlockSpec((TM_FFN, D_MODEL), row),
        out_shape=jax.ShapeDtypeStruct((TOKENS, D_MODEL), F32),
        scratch_shapes=[
            pltpu.VMEM((TM_FFN + SUBLANES, 2 * D_FF), F32),
            pltpu.VMEM((TM_FFN, D_FF), BF16),
        ],
        compiler_params=_params("arbitrary"),
        name="ffn",
    )(x, gpre, wup, cw, cb, wdown, gpost)


def kernel(x, mem, rel_bias, pre_mix_g, w_in, mconv_w, mconv_b, b_igate, b_fgate, attn_out_g,
           mlstm_out_g, w_out, post_mix_g, pre_mem_g, wq_mem, wk_mem, wv_mem, wo_mem, post_mem_g,
           pre_ffn_g, w_up, fconv_w, fconv_b, w_down, post_ffn_g):
    vec3 = lambda t: t.reshape(DEPTH, 1, -1)

    w_main = w_in[:, :, :MAIN_COLS].astype(BF16)
    n_gate = 2 * N_MLSTM_HEADS
    w_gate = jnp.pad(w_in[:, :, MAIN_COLS:], ((0, 0), (0, 0), (0, LANES - n_gate))).astype(BF16)
    gbias = jnp.pad(jnp.concatenate([b_igate, b_fgate], axis=1), ((0, 0), (0, LANES - n_gate)))
    ksc = jnp.concatenate([jnp.ones((1, MLSTM_WIDTH), F32),
                           jnp.full((1, MLSTM_WIDTH), MLSTM_HEAD_DIM ** -0.5, F32)], axis=1)
    w_out_bf = w_out.astype(BF16)
    wq_bf = wq_mem.astype(BF16)
    wo_bf = wo_mem.astype(BF16)
    wkv_bf = jnp.stack([wk_mem, wv_mem], axis=1).astype(BF16)
    w_up_bf = w_up.astype(BF16)
    w_down_bf = w_down.astype(BF16)
    bias = _attn_bias_table(rel_bias)

    kv = _memkv(mem.reshape(BATCH * N_MEM, D_MODEL).astype(BF16), wkv_bf)

    xs = x.reshape(TOKENS, D_MODEL)
    for l in range(DEPTH):
        qkv, qk, v, om, gates = _mix_in(l, xs, vec3(pre_mix_g), w_main, w_gate, mconv_w,
                                        vec3(mconv_b), ksc)
        attn = _dil_attn(qkv, bias)
        hmn = _mlstm(l, qk, v, om, gates, vec3(gbias), vec3(mlstm_out_g))
        xs = _mix_out(l, xs, attn, hmn, vec3(attn_out_g), w_out_bf, vec3(post_mix_g),
                      vec3(pre_mem_g), wq_bf, kv, wo_bf, vec3(post_mem_g))
        xs = _ffn(l, xs, vec3(pre_ffn_g), w_up_bf, fconv_w, vec3(fconv_b), w_down_bf,
                  vec3(post_ffn_g))
    return xs.reshape(BATCH, SEQ, D_MODEL)
```

```python
import functools
import math

import jax
import jax.numpy as jnp
import numpy as np
from jax import lax
from jax.experimental import pallas as pl
from jax.experimental.pallas import tpu as pltpu

F32 = jnp.float32
BF16 = jnp.bfloat16

D_MODEL = 1024
BATCH = 4
SEQ = 4096
DEPTH = 4
TOKENS = BATCH * SEQ

ATTN_HEAD_DIM = 64
ATTN_WIDTH = 512
N_ATTN_HEADS = 8
DILATIONS = (1, 4, 16)
ATTN_BLK = 128
N_MLSTM_HEADS = 4
MLSTM_WIDTH = 512
MLSTM_HEAD_DIM = 128
MLSTM_CHUNK = 128
MLSTM_CONV = 4
N_BUCKETS = 32
MAX_DISTANCE = 2048
N_MEM = 256
N_XHEADS = 4
XHEAD_DIM = 256
D_FF = 2816
FFN_CONV = 3
EPS = 1e-6

LANES = 128
SUBLANES = 8
NEG = -1e30
VMEM_LIMIT = 56 * 1024 * 1024

TM_IN = 512
TM_OUT = 512
TM_FFN = 256
FFN_COLS = 256
MAIN_COLS = 3 * ATTN_WIDTH + 4 * MLSTM_WIDTH

NT = (((1,), (1,)), ((), ()))
TN = (((0,), (0,)), ((), ()))


def _dot(a, b):
    return jnp.dot(a, b, preferred_element_type=F32)


def _dot_nt(a, b):
    return lax.dot_general(a, b, NT, preferred_element_type=F32)


def _rms(x, g):
    return x * lax.rsqrt(jnp.mean(x * x, axis=-1, keepdims=True) + EPS) * g


def _sigmoid(x):
    return 1.0 / (1.0 + jnp.exp(-x))


def _resident(block_shape, index_map):
    return pl.BlockSpec(block_shape, index_map, pipeline_mode=pl.Buffered(1))


def _params(*semantics):
    return pltpu.CompilerParams(dimension_semantics=semantics, vmem_limit_bytes=VMEM_LIMIT)


def _memkv_kernel(mem_ref, w_ref, out_ref):
    out_ref[...] = _dot(mem_ref[...], w_ref[...]).astype(BF16)


def _memkv(mem_bf, wkv_bf):
    rows = BATCH * N_MEM
    return pl.pallas_call(
        _memkv_kernel,
        grid=(DEPTH, 2),
        in_specs=[
            _resident((rows, D_MODEL), lambda l, j: (0, 0)),
            pl.BlockSpec((None, None, D_MODEL, D_MODEL), lambda l, j: (l, j, 0, 0)),
        ],
        out_specs=pl.BlockSpec((None, None, rows, D_MODEL), lambda l, j: (l, j, 0, 0)),
        out_shape=jax.ShapeDtypeStruct((DEPTH, 2, rows, D_MODEL), BF16),
        compiler_params=_params("parallel", "parallel"),
        name="memkv",
    )(mem_bf, wkv_bf)


def _mix_in_kernel(x_ref, g_ref, w_ref, wg_ref, cw_ref, cb_ref, ksc_ref,
                   qkv_ref, qk_ref, v_ref, om_ref, gate_ref, buf_ref):
    @pl.when(pl.program_id(0) % (SEQ // TM_IN) == 0)
    def _():
        buf_ref[0:SUBLANES, :] = jnp.zeros((SUBLANES, 2 * MLSTM_WIDTH), F32)

    h = _rms(x_ref[...], g_ref[...]).astype(BF16)
    a0 = 3 * ATTN_WIDTH
    a1 = a0 + 2 * MLSTM_WIDTH
    a2 = a1 + MLSTM_WIDTH
    qkv_ref[...] = _dot(h, w_ref[:, 0:a0]).astype(BF16)

    pre = _dot(h, w_ref[:, a0:a1])
    buf_ref[SUBLANES:SUBLANES + TM_IN, :] = pre
    y = cb_ref[...]
    for j in range(MLSTM_CONV - 1):
        lo = SUBLANES - (MLSTM_CONV - 1) + j
        y = y + buf_ref[lo:lo + TM_IN, :] * cw_ref[j:j + 1, :]
    y = y + pre * cw_ref[MLSTM_CONV - 1:MLSTM_CONV, :]
    buf_ref[0:SUBLANES, :] = buf_ref[TM_IN:TM_IN + SUBLANES, :]
    qk_ref[...] = (y * _sigmoid(y) * ksc_ref[...]).astype(BF16)

    v_ref[...] = _dot(h, w_ref[:, a1:a2]).astype(BF16)
    om_ref[...] = _dot(h, w_ref[:, a2:MAIN_COLS]).astype(BF16)
    gate_ref[...] = _dot(h, wg_ref[...])


def _mix_in(l, x, pre_g, w_main, w_gate, cw, cb, ksc):
    row = lambda i: (i, 0)
    lay = lambda i: (l, 0, 0)
    return pl.pallas_call(
        _mix_in_kernel,
        grid=(TOKENS // TM_IN,),
        in_specs=[
            pl.BlockSpec((TM_IN, D_MODEL), row),
            _resident((None, 1, D_MODEL), lay),
            _resident((None, D_MODEL, MAIN_COLS), lay),
            _resident((None, D_MODEL, LANES), lay),
            _resident((None, MLSTM_CONV, 2 * MLSTM_WIDTH), lay),
            _resident((None, 1, 2 * MLSTM_WIDTH), lay),
            _resident((1, 2 * MLSTM_WIDTH), lambda i: (0, 0)),
        ],
        out_specs=[
            pl.BlockSpec((TM_IN, 3 * ATTN_WIDTH), row),
            pl.BlockSpec((TM_IN, 2 * MLSTM_WIDTH), row),
            pl.BlockSpec((TM_IN, MLSTM_WIDTH), row),
            pl.BlockSpec((TM_IN, MLSTM_WIDTH), row),
            pl.BlockSpec((TM_IN, LANES), row),
        ],
        out_shape=[
            jax.ShapeDtypeStruct((TOKENS, 3 * ATTN_WIDTH), BF16),
            jax.ShapeDtypeStruct((TOKENS, 2 * MLSTM_WIDTH), BF16),
            jax.ShapeDtypeStruct((TOKENS, MLSTM_WIDTH), BF16),
            jax.ShapeDtypeStruct((TOKENS, MLSTM_WIDTH), BF16),
            jax.ShapeDtypeStruct((TOKENS, LANES), F32),
        ],
        scratch_shapes=[pltpu.VMEM((TM_IN + SUBLANES, 2 * MLSTM_WIDTH), F32)],
        compiler_params=_params("arbitrary"),
        name="mix_in",
    )(x, pre_g, w_main, w_gate, cw, cb, ksc)


N_UNITS = SEQ // ATTN_BLK
UNIT_UNROLL = 8


def _dil_attn_kernel(q_ref, k_ref, v_ref, bias_ref, out_ref,
                     qf, kf, vf, dq, dk, dv, o_s0, o_s1, o_s2, l_s0, l_s1, l_s2):
    conv_rows = 256

    def to_f32(r, carry):
        rows = pl.ds(pl.multiple_of(r * conv_rows, conv_rows), conv_rows)
        qf[rows, :] = q_ref[rows, :].astype(F32) * (ATTN_HEAD_DIM ** -0.5)
        kf[rows, :] = k_ref[rows, :].astype(F32)
        vf[rows, :] = v_ref[rows, :].astype(F32)
        return carry

    lax.fori_loop(0, SEQ // conv_rows, to_f32, 0)

    lane = lax.broadcasted_iota(jnp.int32, (ATTN_BLK, LANES), 1)
    first_head = lane < ATTN_HEAD_DIM
    in_prev_half = lax.broadcasted_iota(jnp.int32, (ATTN_BLK, 2 * ATTN_BLK), 1) < ATTN_BLK

    def run_pattern(p, d, o_scr, l_scr):
        nb = N_UNITS // d
        shift = int(math.log2(nb))

        def start_of(u):
            c = lax.shift_right_logical(u, shift)
            n = lax.bitwise_and(u, nb - 1)
            return c + n * (ATTN_BLK * d), n

        def deinterleave(u, carry):
            start, _ = start_of(u)
            src = pl.ds(start, ATTN_BLK, stride=d)
            dst = pl.ds(pl.multiple_of(u * ATTN_BLK, ATTN_BLK), ATTN_BLK)
            dq[dst, :] = qf[src, :].astype(BF16)
            dk[dst, :] = kf[src, :].astype(BF16)
            dv[dst, :] = vf[src, :].astype(BF16)
            return carry

        lax.fori_loop(0, N_UNITS, deinterleave, 0)

        def unit(u, carry):
            start, n = start_of(u)
            cur = pl.ds(pl.multiple_of(u * ATTN_BLK, ATTN_BLK), ATTN_BLK)
            prev = pl.ds(pl.multiple_of(jnp.maximum(u - 1, 0) * ATTN_BLK, ATTN_BLK), ATTN_BLK)
            q = dq[cur, :]
            kwin = jnp.concatenate([dk[prev, :], dk[cur, :]], axis=0)
            vwin = jnp.concatenate([dv[prev, :], dv[cur, :]], axis=0)
            no_prev = jnp.where(in_prev_half, jnp.where(n > 0, 0.0, NEG), 0.0)
            o_pair, lse_pair = [], []
            for hh in range(2):
                sel = first_head if hh == 0 else jnp.logical_not(first_head)
                qh = jnp.where(sel, q, jnp.zeros_like(q))
                logits = _dot_nt(qh, kwin) + bias_ref[p, hh] + no_prev
                m = jnp.max(logits, axis=1, keepdims=True)
                e = jnp.exp(logits - m)
                s = jnp.sum(e, axis=1, keepdims=True)
                o_pair.append(_dot(e.astype(BF16), vwin) / s)
                lse_pair.append(m + jnp.log(s))
            dst = pl.ds(start, ATTN_BLK, stride=d)
            o_scr[dst, :] = jnp.where(first_head, o_pair[0], o_pair[1])
            l_scr[dst, :] = jnp.where(first_head, lse_pair[0], lse_pair[1])
            return carry

        lax.fori_loop(0, N_UNITS, unit, 0, unroll=UNIT_UNROLL)

    run_pattern(0, DILATIONS[0], o_s0, l_s0)
    run_pattern(1, DILATIONS[1], o_s1, l_s1)
    run_pattern(2, DILATIONS[2], o_s2, l_s2)

    def combine(r, carry):
        rows = pl.ds(pl.multiple_of(r * ATTN_BLK, ATTN_BLK), ATTN_BLK)
        la, lb, lc = l_s0[rows, :], l_s1[rows, :], l_s2[rows, :]
        mx = jnp.maximum(jnp.maximum(la, lb), lc)
        ea, eb, ec = jnp.exp(la - mx), jnp.exp(lb - mx), jnp.exp(lc - mx)
        mixed = (ea * o_s0[rows, :] + eb * o_s1[rows, :] + ec * o_s2[rows, :]) / (ea + eb + ec)
        out_ref[rows, :] = mixed.astype(BF16)
        return carry

    lax.fori_loop(0, N_UNITS, combine, 0)


def _dil_attn(qkv, bias):
    pairs = N_ATTN_HEADS // 2
    blk = (SEQ, LANES)
    f32_seq = pltpu.VMEM((SEQ, LANES), F32)
    bf_seq = pltpu.VMEM((SEQ, LANES), BF16)
    return pl.pallas_call(
        _dil_attn_kernel,
        grid=(BATCH, pairs),
        in_specs=[
            pl.BlockSpec(blk, lambda b, hp: (b, hp)),
            pl.BlockSpec(blk, lambda b, hp: (b, pairs + hp)),
            pl.BlockSpec(blk, lambda b, hp: (b, 2 * pairs + hp)),
            pl.BlockSpec((len(DILATIONS), 2, ATTN_BLK, 2 * ATTN_BLK), lambda b, hp: (0, hp, 0, 0)),
        ],
        out_specs=pl.BlockSpec(blk, lambda b, hp: (b, hp)),
        out_shape=jax.ShapeDtypeStruct((TOKENS, ATTN_WIDTH), BF16),
        scratch_shapes=[f32_seq] * 3 + [bf_seq] * 3 + [f32_seq] * 6,
        compiler_params=_params("parallel", "parallel"),
        name="dil_attn",
    )(qkv, qkv, qkv, bias)


def _t5_causal_bucket(dist):
    max_exact = N_BUCKETS // 2
    d_f = jnp.maximum(dist, 1).astype(F32)
    large = max_exact + (jnp.log(d_f / max_exact) / math.log(MAX_DISTANCE / max_exact)
                         * (N_BUCKETS - max_exact)).astype(jnp.int32)
    large = jnp.minimum(large, N_BUCKETS - 1)
    return jnp.where(dist < max_exact, dist, large)


def _attn_bias_table(rel_bias):
    blk = ATTN_BLK
    period = 3 * blk + 1
    pad_lo = jnp.full((N_ATTN_HEADS, blk - 1), NEG, F32)
    pad_hi = jnp.full((N_ATTN_HEADS, period - 2 * blk), NEG, F32)
    tables = []
    for d in DILATIONS:
        bias_sub = rel_bias[_t5_causal_bucket(jnp.arange(blk + 1) * d)].T.astype(F32)
        r = jnp.concatenate([pad_lo, bias_sub[:, ::-1], pad_hi], axis=1)
        tiled = jnp.broadcast_to(r[:, None, :], (N_ATTN_HEADS, blk, period))
        skew = tiled.reshape(N_ATTN_HEADS, blk * period)[:, :blk * (period - 1)]
        skew = skew.reshape(N_ATTN_HEADS, blk, period - 1)
        tables.append(skew[:, :, blk - 1:3 * blk - 1])
    return jnp.stack(tables)


def _mlstm_kernel(qk_ref, v_ref, om_ref, gate_ref, gbias_ref, gout_ref, out_ref,
                  c_scr, n_scr, m_scr):
    @pl.when(pl.program_id(1) == 0)
    def _():
        c_scr[...] = jnp.zeros_like(c_scr)
        n_scr[...] = jnp.zeros_like(n_scr)
        m_scr[...] = jnp.zeros_like(m_scr)

    nh = N_MLSTM_HEADS
    ch = MLSTM_CHUNK
    row = lax.broadcasted_iota(jnp.int32, (ch, LANES), 0)
    lane = lax.broadcasted_iota(jnp.int32, (ch, LANES), 1)
    causal = row >= lane

    gates = gate_ref[...] + gbias_ref[...]
    logsig = jnp.minimum(gates, 0.0) - jnp.log1p(jnp.exp(-jnp.abs(gates)))
    logf = jnp.where((lane >= nh) & (lane < 2 * nh), logsig, 0.0)
    tri = causal.astype(BF16)
    hi = logf.astype(BF16)
    r1 = logf - hi.astype(F32)
    mid = r1.astype(BF16)
    lo = (r1 - mid.astype(F32)).astype(BF16)
    bcum = _dot(tri, hi) + _dot(tri, mid) + _dot(tri, lo)
    rows_form = jnp.where(lane < nh, gates, bcum).T

    outs = []
    for h in range(nh):
        hs = slice(h * MLSTM_HEAD_DIM, (h + 1) * MLSTM_HEAD_DIM)
        ks = slice(MLSTM_WIDTH + h * MLSTM_HEAD_DIM, MLSTM_WIDTH + (h + 1) * MLSTM_HEAD_DIM)
        b_col = bcum[:, nh + h:nh + h + 1]
        i_col = gates[:, h:h + 1]
        b_row = rows_form[nh + h:nh + h + 1, :]
        i_row = rows_form[h:h + 1, :]
        g = bcum[ch - 1:ch, nh + h:nh + h + 1]
        m_prev = m_scr[h:h + 1, 0:1]
        q_h = qk_ref[:, hs]
        k_h = qk_ref[:, ks]
        v_h = v_ref[:, hs]

        dmat = jnp.where(causal, b_col - b_row + i_row, NEG)
        a = b_col + m_prev
        m_t = jnp.maximum(a, jnp.max(dmat, axis=1, keepdims=True))
        p = jnp.exp(dmat - m_t) * _dot_nt(q_h, k_h)
        inter = jnp.exp(a - m_t)
        c_prev = c_scr[h]
        n_prev = n_scr[h:h + 1, :]
        num = inter * _dot_nt(q_h, c_prev.astype(BF16)) + _dot(p.astype(BF16), v_h)
        den = (inter * jnp.sum(q_h.astype(F32) * n_prev, axis=1, keepdims=True)
               + jnp.sum(p, axis=1, keepdims=True))
        hv = num / jnp.maximum(jnp.abs(den), jnp.exp(-m_t))
        outs.append(_sigmoid(om_ref[:, hs].astype(F32)) * hv)

        w_col = g - b_col + i_col
        m_new = jnp.maximum(g + m_prev, jnp.max(w_col, axis=0, keepdims=True))
        decay = jnp.exp(g + m_prev - m_new)
        wt = jnp.exp(w_col - m_new)
        vw = (v_h.astype(F32) * wt).astype(BF16)
        c_scr[h] = decay * c_prev + lax.dot_general(vw, k_h, TN, preferred_element_type=F32)
        n_scr[h:h + 1, :] = decay * n_prev + jnp.sum(k_h.astype(F32) * wt, axis=0, keepdims=True)
        m_scr[h:h + 1, :] = jnp.broadcast_to(m_new, (1, LANES))

    hm = jnp.concatenate(outs, axis=1)
    out_ref[...] = _rms(hm, gout_ref[...]).astype(BF16)


def _mlstm(l, qk, v, om, gates, gbias, gout):
    nc = SEQ // MLSTM_CHUNK
    row = lambda b, c: (b * nc + c, 0)
    lay = lambda b, c: (l, 0, 0)
    return pl.pallas_call(
        _mlstm_kernel,
        grid=(BATCH, nc),
        in_specs=[
            pl.BlockSpec((MLSTM_CHUNK, 2 * MLSTM_WIDTH), row),
            pl.BlockSpec((MLSTM_CHUNK, MLSTM_WIDTH), row),
            pl.BlockSpec((MLSTM_CHUNK, MLSTM_WIDTH), row),
            pl.BlockSpec((MLSTM_CHUNK, LANES), row),
            _resident((None, 1, LANES), lay),
            _resident((None, 1, MLSTM_WIDTH), lay),
        ],
        out_specs=pl.BlockSpec((MLSTM_CHUNK, MLSTM_WIDTH), row),
        out_shape=jax.ShapeDtypeStruct((TOKENS, MLSTM_WIDTH), BF16),
        scratch_shapes=[
            pltpu.VMEM((N_MLSTM_HEADS, MLSTM_HEAD_DIM, MLSTM_HEAD_DIM), F32),
            pltpu.VMEM((SUBLANES, MLSTM_HEAD_DIM), F32),
            pltpu.VMEM((SUBLANES, LANES), F32),
        ],
        compiler_params=_params("arbitrary", "arbitrary"),
        name="mlstm",
    )(qk, v, om, gates, gbias, gout)


def _mix_out_kernel(x_ref, attn_ref, hmn_ref, gattn_ref, wout_ref, gpost_ref,
                    gpre_ref, wq_ref, k_ref, v_ref, wo_ref, gpostm_ref, out_ref):
    an = _rms(attn_ref[...].astype(F32), gattn_ref[...]).astype(BF16)
    h = _dot(an, wout_ref[0:ATTN_WIDTH, :]) + _dot(hmn_ref[...], wout_ref[ATTN_WIDTH:, :])
    x1 = x_ref[...] + _rms(h, gpost_ref[...])

    hq = _rms(x1, gpre_ref[...]).astype(BF16)
    q = (_dot(hq, wq_ref[...]) * (XHEAD_DIM ** -0.5)).astype(BF16)
    heads = []
    for hd in range(N_XHEADS):
        sl = slice(hd * XHEAD_DIM, (hd + 1) * XHEAD_DIM)
        logits = _dot_nt(q[:, sl], k_ref[:, sl])
        m = jnp.max(logits, axis=1, keepdims=True)
        e = jnp.exp(logits - m)
        s = jnp.sum(e, axis=1, keepdims=True)
        heads.append((_dot(e.astype(BF16), v_ref[:, sl]) / s).astype(BF16))
    o = jnp.concatenate(heads, axis=1)
    out_ref[...] = x1 + _rms(_dot(o, wo_ref[...]), gpostm_ref[...])


def _mix_out(l, x, attn, hmn, gattn, wout, gpost, gpre, wq, kv, wo, gpostm):
    row = lambda i: (i, 0)
    lay = lambda i: (l, 0, 0)
    tiles_per_seq = SEQ // TM_OUT
    vec = _resident((None, 1, D_MODEL), lay)
    mat = _resident((None, D_MODEL, D_MODEL), lay)
    return pl.pallas_call(
        _mix_out_kernel,
        grid=(TOKENS // TM_OUT,),
        in_specs=[
            pl.BlockSpec((TM_OUT, D_MODEL), row),
            pl.BlockSpec((TM_OUT, ATTN_WIDTH), row),
            pl.BlockSpec((TM_OUT, MLSTM_WIDTH), row),
            _resident((None, 1, ATTN_WIDTH), lay),
            mat, vec, vec, mat,
            pl.BlockSpec((None, None, N_MEM, D_MODEL), lambda i: (l, 0, i // tiles_per_seq, 0)),
            pl.BlockSpec((None, None, N_MEM, D_MODEL), lambda i: (l, 1, i // tiles_per_seq, 0)),
            mat, vec,
        ],
        out_specs=pl.BlockSpec((TM_OUT, D_MODEL), row),
        out_shape=jax.ShapeDtypeStruct((TOKENS, D_MODEL), F32),
        compiler_params=_params("parallel"),
        name="mix_out",
    )(x, attn, hmn, gattn, wout, gpost, gpre, wq, kv, kv, wo, gpostm)


def _ffn_kernel(x_ref, gpre_ref, wup_ref, cw_ref, cb_ref, wdown_ref, gpost_ref, out_ref,
                buf_ref, act_ref):
    @pl.when(pl.program_id(0) % (SEQ // TM_FFN) == 0)
    def _():
        buf_ref[0:SUBLANES, :] = jnp.zeros((SUBLANES, 2 * D_FF), F32)

    x = x_ref[...]
    h = _rms(x, gpre_ref[...]).astype(BF16)

    def conv_cols(c0):
        cols = slice(c0, c0 + FFN_COLS)
        u = _dot(h, wup_ref[:, cols])
        buf_ref[SUBLANES:SUBLANES + TM_FFN, cols] = u
        y = cb_ref[:, cols]
        for j in range(FFN_CONV - 1):
            lo = SUBLANES - (FFN_CONV - 1) + j
            y = y + buf_ref[lo:lo + TM_FFN, cols] * cw_ref[j:j + 1, cols]
        return y + u * cw_ref[FFN_CONV - 1:FFN_CONV, cols]

    for c in range(D_FF // FFN_COLS):
        a = conv_cols(c * FFN_COLS)
        g = conv_cols(D_FF + c * FFN_COLS)
        gelu = 0.5 * g * (1.0 + jnp.tanh(math.sqrt(2.0 / math.pi) * (g + 0.044715 * (g * g * g))))
        act_ref[:, c * FFN_COLS:(c + 1) * FFN_COLS] = (gelu * a).astype(BF16)
    buf_ref[0:SUBLANES, :] = buf_ref[TM_FFN:TM_FFN + SUBLANES, :]

    out_ref[...] = x + _rms(_dot(act_ref[...], wdown_ref[...]), gpost_ref[...])


def _ffn(l, x, gpre, wup, cw, cb, wdown, gpost):
    row = lambda i: (i, 0)
    lay = lambda i: (l, 0, 0)
    vec = _resident((None, 1, D_MODEL), lay)
    return pl.pallas_call(
        _ffn_kernel,
        grid=(TOKENS // TM_FFN,),
        in_specs=[
            pl.BlockSpec((TM_FFN, D_MODEL), row),
            vec,
            _resident((None, D_MODEL, 2 * D_FF), lay),
            _resident((None, FFN_CONV, 2 * D_FF), lay),
            _resident((None, 1, 2 * D_FF), lay),
            _resident((None, D_FF, D_MODEL), lay),
            vec,
        ],
        out_specs=pl.BlockSpec((TM_FFN, D_MODEL), row),
        out_shape=jax.ShapeDtypeStruct((TOKENS, D_MODEL), F32),
        scratch_shapes=[
            pltpu.VMEM((TM_FFN + SUBLANES, 2 * D_FF), F32),
            pltpu.VMEM((TM_FFN, D_FF), BF16),
        ],
        compiler_params=_params("arbitrary"),
        name="ffn",
    )(x, gpre, wup, cw, cb, wdown, gpost)


def kernel(x, mem, rel_bias, pre_mix_g, w_in, mconv_w, mconv_b, b_igate, b_fgate, attn_out_g,
           mlstm_out_g, w_out, post_mix_g, pre_mem_g, wq_mem, wk_mem, wv_mem, wo_mem, post_mem_g,
           pre_ffn_g, w_up, fconv_w, fconv_b, w_down, post_ffn_g):
    vec3 = lambda t: t.reshape(DEPTH, 1, -1)

    w_main = w_in[:, :, :MAIN_COLS].astype(BF16)
    n_gate = 2 * N_MLSTM_HEADS
    w_gate = jnp.pad(w_in[:, :, MAIN_COLS:], ((0, 0), (0, 0), (0, LANES - n_gate))).astype(BF16)
    gbias = jnp.pad(jnp.concatenate([b_igate, b_fgate], axis=1), ((0, 0), (0, LANES - n_gate)))
    ksc = jnp.concatenate([jnp.ones((1, MLSTM_WIDTH), F32),
                           jnp.full((1, MLSTM_WIDTH), MLSTM_HEAD_DIM ** -0.5, F32)], axis=1)
    w_out_bf = w_out.astype(BF16)
    wq_bf = wq_mem.astype(BF16)
    wo_bf = wo_mem.astype(BF16)
    wkv_bf = jnp.stack([wk_mem, wv_mem], axis=1).astype(BF16)
    w_up_bf = w_up.astype(BF16)
    w_down_bf = w_down.astype(BF16)
    bias = _attn_bias_table(rel_bias)

    kv = _memkv(mem.reshape(BATCH * N_MEM, D_MODEL).astype(BF16), wkv_bf)

    xs = x.reshape(TOKENS, D_MODEL)
    for l in range(DEPTH):
        qkv, qk, v, om, gates = _mix_in(l, xs, vec3(pre_mix_g), w_main, w_gate, mconv_w,
                                        vec3(mconv_b), ksc)
        attn = _dil_attn(qkv, bias)
        hmn = _mlstm(l, qk, v, om, gates, vec3(gbias), vec3(mlstm_out_g))
        xs = _mix_out(l, xs, attn, hmn, vec3(attn_out_g), w_out_bf, vec3(post_mix_g),
                      vec3(pre_mem_g), wq_bf, kv, wo_bf, vec3(post_mem_g))
        xs = _ffn(l, xs, vec3(pre_ffn_g), w_up_bf, fconv_w, vec3(fconv_b), w_down_bf,
                  vec3(post_ffn_g))
    return xs.reshape(BATCH, SEQ, D_MODEL)
```

```python
import functools
import math

import jax
import jax.numpy as jnp
import numpy as np
from jax import lax
from jax.experimental import pallas as pl
from jax.experimental.pallas import tpu as pltpu

F32 = jnp.float32
BF16 = jnp.bfloat16

D_MODEL = 1024
BATCH = 4
SEQ = 4096
DEPTH = 4
TOKENS = BATCH * SEQ

ATTN_HEAD_DIM = 64
ATTN_WIDTH = 512
N_ATTN_HEADS = 8
DILATIONS = (1, 4, 16)
ATTN_BLK = 128
N_MLSTM_HEADS = 4
MLSTM_WIDTH = 512
MLSTM_HEAD_DIM = 128
MLSTM_CHUNK = 128
MLSTM_CONV = 4
N_BUCKETS = 32
MAX_DISTANCE = 2048
N_MEM = 256
N_XHEADS = 4
XHEAD_DIM = 256
D_FF = 2816
FFN_CONV = 3
EPS = 1e-6

LANES = 128
SUBLANES = 8
NEG = -1e30
VMEM_LIMIT = 56 * 1024 * 1024

TM_IN = 512
SUB_IN = 256
MIXIN_COLS = 256
TM_OUT = 1024
SUB_OUT = 256
TM_FFN = 256
FFN_COLS = 256
MAIN_COLS = 3 * ATTN_WIDTH + 4 * MLSTM_WIDTH

NT = (((1,), (1,)), ((), ()))
TN = (((0,), (0,)), ((), ()))


def _dot(a, b):
    return jnp.dot(a, b, preferred_element_type=F32)


def _dot_nt(a, b):
    return lax.dot_general(a, b, NT, preferred_element_type=F32)


def _rms(x, g):
    return x * lax.rsqrt(jnp.mean(x * x, axis=-1, keepdims=True) + EPS) * g


def _sigmoid(x):
    return 1.0 / (1.0 + jnp.exp(-x))


def _interleave(gens):
    pending = list(gens)
    active = []
    while pending or active:
        if pending:
            active.append(pending.pop(0))
        for g in list(active):
            try:
                next(g)
            except StopIteration:
                active.remove(g)


def _resident(block_shape, index_map):
    return pl.BlockSpec(block_shape, index_map, pipeline_mode=pl.Buffered(1))


def _params(*semantics):
    return pltpu.CompilerParams(dimension_semantics=semantics, vmem_limit_bytes=VMEM_LIMIT)


def _memkv_kernel(mem_ref, w_ref, out_ref):
    out_ref[...] = _dot(mem_ref[...], w_ref[...]).astype(BF16)


def _memkv(mem_bf, wkv_bf):
    rows = BATCH * N_MEM
    return pl.pallas_call(
        _memkv_kernel,
        grid=(DEPTH, 2),
        in_specs=[
            _resident((rows, D_MODEL), lambda l, j: (0, 0)),
            pl.BlockSpec((None, None, D_MODEL, D_MODEL), lambda l, j: (l, j, 0, 0)),
        ],
        out_specs=pl.BlockSpec((None, None, rows, D_MODEL), lambda l, j: (l, j, 0, 0)),
        out_shape=jax.ShapeDtypeStruct((DEPTH, 2, rows, D_MODEL), BF16),
        compiler_params=_params("parallel", "parallel"),
        name="memkv",
    )(mem_bf, wkv_bf)


def _mix_in_kernel(x_ref, g_ref, w_ref, wg_ref, cw_ref, cb_ref, ksc_ref,
                   qkv_ref, qk_ref, v_ref, om_ref, gate_ref, buf_ref):
    @pl.when(pl.program_id(0) % (SEQ // TM_IN) == 0)
    def _():
        buf_ref[0:SUBLANES, :] = jnp.zeros((SUBLANES, 2 * MLSTM_WIDTH), F32)

    a0 = 3 * ATTN_WIDTH
    a1 = a0 + 2 * MLSTM_WIDTH
    a2 = a1 + MLSTM_WIDTH
    cw_cols = MIXIN_COLS
    n_conv = 2 * MLSTM_WIDTH // cw_cols

    def sub_tile(r0):
        rows = slice(r0, r0 + SUB_IN)
        h = _rms(x_ref[rows, :], g_ref[...]).astype(BF16)
        yield

        def plain(out_ref, w0, o0):
            out_ref[rows, o0:o0 + cw_cols] = _dot(h, w_ref[:, w0:w0 + cw_cols]).astype(BF16)

        plain_jobs = ([(qkv_ref, c, c) for c in range(0, a0, cw_cols)]
                      + [(v_ref, a1 + c, c) for c in range(0, MLSTM_WIDTH, cw_cols)]
                      + [(om_ref, a2 + c, c) for c in range(0, MLSTM_WIDTH, cw_cols)])
        per_conv = -(-len(plain_jobs) // n_conv)
        for cc in range(n_conv):
            cols = slice(cc * cw_cols, (cc + 1) * cw_cols)
            pre = _dot(h, w_ref[:, a0 + cc * cw_cols:a0 + (cc + 1) * cw_cols])
            buf_ref[SUBLANES + r0:SUBLANES + r0 + SUB_IN, cols] = pre
            yield
            for job in plain_jobs[cc * per_conv:(cc + 1) * per_conv]:
                plain(*job)
            y = cb_ref[:, cols]
            for j in range(MLSTM_CONV - 1):
                lo = SUBLANES + r0 - (MLSTM_CONV - 1) + j
                y = y + buf_ref[lo:lo + SUB_IN, cols] * cw_ref[j:j + 1, cols]
            y = y + pre * cw_ref[MLSTM_CONV - 1:MLSTM_CONV, cols]
            qk_ref[rows, cols] = (y * _sigmoid(y) * ksc_ref[:, cols]).astype(BF16)
            yield
        gate_ref[rows, :] = _dot(h, wg_ref[...])

    _interleave([sub_tile(r0) for r0 in range(0, TM_IN, SUB_IN)])
    buf_ref[0:SUBLANES, :] = buf_ref[TM_IN:TM_IN + SUBLANES, :]


def _mix_in(l, x, pre_g, w_main, w_gate, cw, cb, ksc):
    row = lambda i: (i, 0)
    lay = lambda i: (l, 0, 0)
    return pl.pallas_call(
        _mix_in_kernel,
        grid=(TOKENS // TM_IN,),
        in_specs=[
            pl.BlockSpec((TM_IN, D_MODEL), row),
            _resident((None, 1, D_MODEL), lay),
            _resident((None, D_MODEL, MAIN_COLS), lay),
            _resident((None, D_MODEL, LANES), lay),
            _resident((None, MLSTM_CONV, 2 * MLSTM_WIDTH), lay),
            _resident((None, 1, 2 * MLSTM_WIDTH), lay),
            _resident((1, 2 * MLSTM_WIDTH), lambda i: (0, 0)),
        ],
        out_specs=[
            pl.BlockSpec((TM_IN, 3 * ATTN_WIDTH), row),
            pl.BlockSpec((TM_IN, 2 * MLSTM_WIDTH), row),
            pl.BlockSpec((TM_IN, MLSTM_WIDTH), row),
            pl.BlockSpec((TM_IN, MLSTM_WIDTH), row),
            pl.BlockSpec((TM_IN, LANES), row),
        ],
        out_shape=[
            jax.ShapeDtypeStruct((TOKENS, 3 * ATTN_WIDTH), BF16),
            jax.ShapeDtypeStruct((TOKENS, 2 * MLSTM_WIDTH), BF16),
            jax.ShapeDtypeStruct((TOKENS, MLSTM_WIDTH), BF16),
            jax.ShapeDtypeStruct((TOKENS, MLSTM_WIDTH), BF16),
            jax.ShapeDtypeStruct((TOKENS, LANES), F32),
        ],
        scratch_shapes=[pltpu.VMEM((TM_IN + SUBLANES, 2 * MLSTM_WIDTH), F32)],
        compiler_params=_params("arbitrary"),
        name="mix_in",
    )(x, pre_g, w_main, w_gate, cw, cb, ksc)


N_UNITS = SEQ // ATTN_BLK
PIPE_DEPTH = 4
ONES_ROWS = 16


def _div(u, n):
    return lax.shift_right_logical(u, int(math.log2(n)))


def _mod(u, n):
    return lax.bitwise_and(u, n - 1)


def _dil_attn_kernel(q_ref, k_ref, v_ref, bias_ref, out_ref,
                     qf, kf, vf, q4f, k4f, v4f, dq, dk, dvt, o_s0, o_s1, o_s2, l_s0, l_s1, l_s2):
    d1, d4, d16 = DILATIONS
    hd = ATTN_HEAD_DIM
    dvt[LANES:LANES + ONES_ROWS, :] = jnp.ones((ONES_ROWS, SEQ), BF16)

    def widen(r, carry):
        rows = pl.ds(pl.multiple_of(r * ATTN_BLK, ATTN_BLK), ATTN_BLK)
        q = q_ref[rows, :].astype(F32) * (hd ** -0.5)
        k = k_ref[rows, :].astype(F32)
        v = v_ref[rows, :].astype(F32)
        qf[rows, :] = q
        kf[rows, :] = k
        vf[rows, :] = v
        dq[rows, :] = q.astype(BF16)
        dk[rows, :] = k.astype(BF16)
        dvt[0:LANES, rows] = v.T.astype(BF16)
        return carry

    lax.fori_loop(0, N_UNITS, widen, 0, unroll=4)

    lane = lax.broadcasted_iota(jnp.int32, (ATTN_BLK, LANES), 1)
    first_head = lane < hd

    def deinterleave(src_refs, keep_refs, stride, start_of):
        def body(u, carry):
            src = pl.ds(start_of(u), ATTN_BLK, stride=stride)
            dst = pl.ds(pl.multiple_of(u * ATTN_BLK, ATTN_BLK), ATTN_BLK)
            q, k, v = (r[src, :] for r in src_refs)
            if keep_refs is not None:
                for r, val in zip(keep_refs, (q, k, v)):
                    r[dst, :] = val
            dq[dst, :] = q.astype(BF16)
            dk[dst, :] = k.astype(BF16)
            dvt[0:LANES, dst] = v.T.astype(BF16)
            return carry
        lax.fori_loop(0, N_UNITS, body, 0, unroll=4)

    def run_pattern(p, d, o_scr, l_scr):
        nb = N_UNITS // d

        def qk(u):
            cur = pl.ds(u * ATTN_BLK, ATTN_BLK)
            prev = pl.ds(max(u - 1, 0) * ATTN_BLK, ATTN_BLK)
            q = dq[cur, :]
            zero = jnp.zeros_like(q)
            q2 = jnp.concatenate([jnp.where(first_head, q, zero), jnp.where(first_head, zero, q)], axis=0)
            kwin = jnp.concatenate([dk[prev, :], dk[cur, :]], axis=0)
            return _dot_nt(kwin, q2) + bias_ref[p, 0 if u % nb > 0 else 1]

        def rest(u, st):
            cur = pl.ds(u * ATTN_BLK, ATTN_BLK)
            prev = pl.ds(max(u - 1, 0) * ATTN_BLK, ATTN_BLK)
            m = jnp.max(st, axis=0, keepdims=True)
            e = jnp.exp(st - m)
            vtwin = jnp.concatenate([dvt[:, prev], dvt[:, cur]], axis=1)
            ot = _dot(vtwin, e.astype(BF16))
            s = ot[LANES:LANES + 1, :]
            inv = 1.0 / s
            lse = m + jnp.log(s)
            o_t = jnp.concatenate([ot[0:hd, 0:ATTN_BLK] * inv[:, 0:ATTN_BLK],
                                   ot[hd:2 * hd, ATTN_BLK:] * inv[:, ATTN_BLK:]], axis=0)
            l_t = jnp.concatenate([jnp.broadcast_to(lse[:, 0:ATTN_BLK], (hd, ATTN_BLK)),
                                   jnp.broadcast_to(lse[:, ATTN_BLK:], (hd, ATTN_BLK))], axis=0)
            dst = pl.ds(u // nb + (u % nb) * (ATTN_BLK * d), ATTN_BLK, stride=d)
            o_scr[dst, :] = o_t.T
            l_scr[dst, :] = l_t.T

        pending = {}
        for u in range(N_UNITS + PIPE_DEPTH):
            if u < N_UNITS:
                pending[u] = qk(u)
            if u >= PIPE_DEPTH:
                rest(u - PIPE_DEPTH, pending.pop(u - PIPE_DEPTH))

    run_pattern(0, d1, o_s0, l_s0)
    per4 = SEQ // d4
    nb4 = N_UNITS // d4
    deinterleave((qf, kf, vf), (q4f, k4f, v4f), d4,
                 lambda u: _div(u, nb4) + _mod(u, nb4) * (ATTN_BLK * d4))
    run_pattern(1, d4, o_s1, l_s1)
    ratio = d16 // d4
    nb16 = N_UNITS // d16
    deinterleave((q4f, k4f, v4f), None, ratio,
                 lambda u: (_mod(_div(u, nb16), d4) * per4 + _mod(u, nb16) * (ATTN_BLK * ratio)
                            + _div(_div(u, nb16), d4)))
    run_pattern(2, d16, o_s2, l_s2)

    def combine(r, carry):
        rows = pl.ds(pl.multiple_of(r * ATTN_BLK, ATTN_BLK), ATTN_BLK)
        la, lb, lc = l_s0[rows, :], l_s1[rows, :], l_s2[rows, :]
        mx = jnp.maximum(jnp.maximum(la, lb), lc)
        ea, eb, ec = jnp.exp(la - mx), jnp.exp(lb - mx), jnp.exp(lc - mx)
        mixed = (ea * o_s0[rows, :] + eb * o_s1[rows, :] + ec * o_s2[rows, :]) / (ea + eb + ec)
        out_ref[rows, :] = mixed.astype(BF16)
        return carry

    lax.fori_loop(0, N_UNITS, combine, 0, unroll=2)


def _dil_attn(qkv, bias):
    pairs = N_ATTN_HEADS // 2
    blk = (SEQ, LANES)
    f32_seq = pltpu.VMEM((SEQ, LANES), F32)
    bf_seq = pltpu.VMEM((SEQ, LANES), BF16)
    return pl.pallas_call(
        _dil_attn_kernel,
        grid=(BATCH, pairs),
        in_specs=[
            pl.BlockSpec(blk, lambda b, hp: (b, hp)),
            pl.BlockSpec(blk, lambda b, hp: (b, pairs + hp)),
            pl.BlockSpec(blk, lambda b, hp: (b, 2 * pairs + hp)),
            pl.BlockSpec((None, len(DILATIONS), 2, 2 * ATTN_BLK, 2 * ATTN_BLK),
                         lambda b, hp: (hp, 0, 0, 0, 0)),
        ],
        out_specs=pl.BlockSpec(blk, lambda b, hp: (b, hp)),
        out_shape=jax.ShapeDtypeStruct((TOKENS, ATTN_WIDTH), BF16),
        scratch_shapes=([f32_seq] * 6 + [bf_seq] * 2 + [pltpu.VMEM((LANES + ONES_ROWS, SEQ), BF16)]
                        + [f32_seq] * 6),
        compiler_params=_params("parallel", "parallel"),
        name="dil_attn",
    )(qkv, qkv, qkv, bias)


def _t5_causal_bucket(dist):
    max_exact = N_BUCKETS // 2
    d_f = jnp.maximum(dist, 1).astype(F32)
    large = max_exact + (jnp.log(d_f / max_exact) / math.log(MAX_DISTANCE / max_exact)
                         * (N_BUCKETS - max_exact)).astype(jnp.int32)
    large = jnp.minimum(large, N_BUCKETS - 1)
    return jnp.where(dist < max_exact, dist, large)


def _attn_bias_table(rel_bias):
    pairs = N_ATTN_HEADS // 2
    n_pat = len(DILATIONS)
    t = _attn_bias_qk(rel_bias).reshape(n_pat, pairs, 2, ATTN_BLK, 2 * ATTN_BLK)
    t = jnp.transpose(t, (1, 0, 4, 2, 3)).reshape(pairs, n_pat, 2 * ATTN_BLK, 2 * ATTN_BLK)
    key_in_prev = (jnp.arange(2 * ATTN_BLK) < ATTN_BLK)[None, None, :, None]
    return jnp.stack([t, jnp.where(key_in_prev, NEG, t)], axis=2)


def _attn_bias_qk(rel_bias):
    blk = ATTN_BLK
    period = 3 * blk + 1
    pad_lo = jnp.full((N_ATTN_HEADS, blk - 1), NEG, F32)
    pad_hi = jnp.full((N_ATTN_HEADS, period - 2 * blk), NEG, F32)
    tables = []
    for d in DILATIONS:
        bias_sub = rel_bias[_t5_causal_bucket(jnp.arange(blk + 1) * d)].T.astype(F32)
        r = jnp.concatenate([pad_lo, bias_sub[:, ::-1], pad_hi], axis=1)
        tiled = jnp.broadcast_to(r[:, None, :], (N_ATTN_HEADS, blk, period))
        skew = tiled.reshape(N_ATTN_HEADS, blk * period)[:, :blk * (period - 1)]
        skew = skew.reshape(N_ATTN_HEADS, blk, period - 1)
        tables.append(skew[:, :, blk - 1:3 * blk - 1])
    return jnp.stack(tables)


def _mlstm_kernel(qk_ref, v_ref, om_ref, gate_ref, gbias_ref, gout_ref, out_ref,
                  c_scr, n_scr, m_scr):
    @pl.when(pl.program_id(1) == 0)
    def _():
        c_scr[...] = jnp.zeros_like(c_scr)
        n_scr[...] = jnp.zeros_like(n_scr)
        m_scr[...] = jnp.zeros_like(m_scr)

    nh = N_MLSTM_HEADS
    ch = MLSTM_CHUNK
    row = lax.broadcasted_iota(jnp.int32, (ch, LANES), 0)
    lane = lax.broadcasted_iota(jnp.int32, (ch, LANES), 1)
    causal = row >= lane

    gates = gate_ref[...] + gbias_ref[...]
    logsig = jnp.minimum(gates, 0.0) - jnp.log1p(jnp.exp(-jnp.abs(gates)))
    logf = jnp.where((lane >= nh) & (lane < 2 * nh), logsig, 0.0)
    tri = causal.astype(BF16)
    hi = logf.astype(BF16)
    r1 = logf - hi.astype(F32)
    mid = r1.astype(BF16)
    lo = (r1 - mid.astype(F32)).astype(BF16)
    bcum = _dot(tri, hi) + _dot(tri, mid) + _dot(tri, lo)
    rows_form = jnp.where(lane < nh, gates, bcum).T

    outs = []
    for h in range(nh):
        hs = slice(h * MLSTM_HEAD_DIM, (h + 1) * MLSTM_HEAD_DIM)
        ks = slice(MLSTM_WIDTH + h * MLSTM_HEAD_DIM, MLSTM_WIDTH + (h + 1) * MLSTM_HEAD_DIM)
        b_col = bcum[:, nh + h:nh + h + 1]
        i_col = gates[:, h:h + 1]
        b_row = rows_form[nh + h:nh + h + 1, :]
        i_row = rows_form[h:h + 1, :]
        g = bcum[ch - 1:ch, nh + h:nh + h + 1]
        m_prev = m_scr[h:h + 1, 0:1]
        q_h = qk_ref[:, hs]
        k_h = qk_ref[:, ks]
        v_h = v_ref[:, hs]

        dmat = jnp.where(causal, b_col - b_row + i_row, NEG)
        a = b_col + m_prev
        m_t = jnp.maximum(a, jnp.max(dmat, axis=1, keepdims=True))
        p = jnp.exp(dmat - m_t) * _dot_nt(q_h, k_h)
        inter = jnp.exp(a - m_t)
        c_prev = c_scr[h]
        n_prev = n_scr[h:h + 1, :]
        num = inter * _dot_nt(q_h, c_prev.astype(BF16)) + _dot(p.astype(BF16), v_h)
        den = (inter * jnp.sum(q_h.astype(F32) * n_prev, axis=1, keepdims=True)
               + jnp.sum(p, axis=1, keepdims=True))
        hv = num / jnp.maximum(jnp.abs(den), jnp.exp(-m_t))
        outs.append(_sigmoid(om_ref[:, hs].astype(F32)) * hv)

        w_col = g - b_col + i_col
        m_new = jnp.maximum(g + m_prev, jnp.max(w_col, axis=0, keepdims=True))
        decay = jnp.exp(g + m_prev - m_new)
        wt = jnp.exp(w_col - m_new)
        vw = (v_h.astype(F32) * wt).astype(BF16)
        c_scr[h] = decay * c_prev + lax.dot_general(vw, k_h, TN, preferred_element_type=F32)
        n_scr[h:h + 1, :] = decay * n_prev + jnp.sum(k_h.astype(F32) * wt, axis=0, keepdims=True)
        m_scr[h:h + 1, :] = jnp.broadcast_to(m_new, (1, LANES))

    hm = jnp.concatenate(outs, axis=1)
    out_ref[...] = _rms(hm, gout_ref[...]).astype(BF16)


def _mlstm(l, qk, v, om, gates, gbias, gout):
    nc = SEQ // MLSTM_CHUNK
    row = lambda b, c: (b * nc + c, 0)
    lay = lambda b, c: (l, 0, 0)
    return pl.pallas_call(
        _mlstm_kernel,
        grid=(BATCH, nc),
        in_specs=[
            pl.BlockSpec((MLSTM_CHUNK, 2 * MLSTM_WIDTH), row),
            pl.BlockSpec((MLSTM_CHUNK, MLSTM_WIDTH), row),
            pl.BlockSpec((MLSTM_CHUNK, MLSTM_WIDTH), row),
            pl.BlockSpec((MLSTM_CHUNK, LANES), row),
            _resident((None, 1, LANES), lay),
            _resident((None, 1, MLSTM_WIDTH), lay),
        ],
        out_specs=pl.BlockSpec((MLSTM_CHUNK, MLSTM_WIDTH), row),
        out_shape=jax.ShapeDtypeStruct((TOKENS, MLSTM_WIDTH), BF16),
        scratch_shapes=[
            pltpu.VMEM((N_MLSTM_HEADS, MLSTM_HEAD_DIM, MLSTM_HEAD_DIM), F32),
            pltpu.VMEM((SUBLANES, MLSTM_HEAD_DIM), F32),
            pltpu.VMEM((SUBLANES, LANES), F32),
        ],
        compiler_params=_params("arbitrary", "arbitrary"),
        name="mlstm",
    )(qk, v, om, gates, gbias, gout)


def _mix_out_kernel(x_ref, attn_ref, hmn_ref, gattn_ref, wout_ref, gpost_ref,
                    gpre_ref, wq_ref, k_ref, v_ref, wo_ref, gpostm_ref, out_ref):
    def sub_tile(r0):
        rows = slice(r0, r0 + SUB_OUT)
        an = _rms(attn_ref[rows, :].astype(F32), gattn_ref[...]).astype(BF16)
        yield
        h = _dot(an, wout_ref[0:ATTN_WIDTH, :]) + _dot(hmn_ref[rows, :], wout_ref[ATTN_WIDTH:, :])
        yield
        x1 = x_ref[rows, :] + _rms(h, gpost_ref[...])
        hq = _rms(x1, gpre_ref[...]).astype(BF16)
        yield
        q = (_dot(hq, wq_ref[...]) * (XHEAD_DIM ** -0.5)).astype(BF16)
        yield
        heads = []
        for hd in range(N_XHEADS):
            sl = slice(hd * XHEAD_DIM, (hd + 1) * XHEAD_DIM)
            logits = _dot_nt(q[:, sl], k_ref[:, sl])
            yield
            m = jnp.max(logits, axis=1, keepdims=True)
            e = jnp.exp(logits - m)
            s = jnp.sum(e, axis=1, keepdims=True)
            yield
            heads.append((_dot(e.astype(BF16), v_ref[:, sl]) / s).astype(BF16))
        o = jnp.concatenate(heads, axis=1)
        yield
        h2 = _dot(o, wo_ref[...])
        yield
        out_ref[rows, :] = x1 + _rms(h2, gpostm_ref[...])

    _interleave([sub_tile(r0) for r0 in range(0, TM_OUT, SUB_OUT)])


def _mix_out(l, x, attn, hmn, gattn, wout, gpost, gpre, wq, kv, wo, gpostm):
    row = lambda i: (i, 0)
    lay = lambda i: (l, 0, 0)
    tiles_per_seq = SEQ // TM_OUT
    vec = _resident((None, 1, D_MODEL), lay)
    mat = _resident((None, D_MODEL, D_MODEL), lay)
    return pl.pallas_call(
        _mix_out_kernel,
        grid=(TOKENS // TM_OUT,),
        in_specs=[
            pl.BlockSpec((TM_OUT, D_MODEL), row),
            pl.BlockSpec((TM_OUT, ATTN_WIDTH), row),
            pl.BlockSpec((TM_OUT, MLSTM_WIDTH), row),
            _resident((None, 1, ATTN_WIDTH), lay),
            mat, vec, vec, mat,
            pl.BlockSpec((None, None, N_MEM, D_MODEL), lambda i: (l, 0, i // tiles_per_seq, 0)),
            pl.BlockSpec((None, None, N_MEM, D_MODEL), lambda i: (l, 1, i // tiles_per_seq, 0)),
            mat, vec,
        ],
        out_specs=pl.BlockSpec((TM_OUT, D_MODEL), row),
        out_shape=jax.ShapeDtypeStruct((TOKENS, D_MODEL), F32),
        compiler_params=_params("parallel"),
        name="mix_out",
    )(x, attn, hmn, gattn, wout, gpost, gpre, wq, kv, kv, wo, gpostm)


def _ffn_kernel(x_ref, gpre_ref, wup_ref, cw_ref, cb_ref, wdown_ref, gpost_ref, out_ref,
                buf_ref, act_ref):
    @pl.when(pl.program_id(0) % (SEQ // TM_FFN) == 0)
    def _():
        buf_ref[0:SUBLANES, :] = jnp.zeros((SUBLANES, 2 * D_FF), F32)

    x = x_ref[...]
    h = _rms(x, gpre_ref[...]).astype(BF16)

    def conv_cols(c0):
        cols = slice(c0, c0 + FFN_COLS)
        u = _dot(h, wup_ref[:, cols])
        buf_ref[SUBLANES:SUBLANES + TM_FFN, cols] = u
        y = cb_ref[:, cols]
        for j in range(FFN_CONV - 1):
            lo = SUBLANES - (FFN_CONV - 1) + j
            y = y + buf_ref[lo:lo + TM_FFN, cols] * cw_ref[j:j + 1, cols]
        return y + u * cw_ref[FFN_CONV - 1:FFN_CONV, cols]

    for c in range(D_FF // FFN_COLS):
        a = conv_cols(c * FFN_COLS)
        g = conv_cols(D_FF + c * FFN_COLS)
        gelu = 0.5 * g * (1.0 + jnp.tanh(math.sqrt(2.0 / math.pi) * (g + 0.044715 * (g * g * g))))
        act_ref[:, c * FFN_COLS:(c + 1) * FFN_COLS] = (gelu * a).astype(BF16)
    buf_ref[0:SUBLANES, :] = buf_ref[TM_FFN:TM_FFN + SUBLANES, :]

    out_ref[...] = x + _rms(_dot(act_ref[...], wdown_ref[...]), gpost_ref[...])


def _ffn(l, x, gpre, wup, cw, cb, wdown, gpost):
    row = lambda i: (i, 0)
    lay = lambda i: (l, 0, 0)
    vec = _resident((None, 1, D_MODEL), lay)
    return pl.pallas_call(
        _ffn_kernel,
        grid=(TOKENS // TM_FFN,),
        in_specs=[
            pl.BlockSpec((TM_FFN, D_MODEL), row),
            vec,
            _resident((None, D_MODEL, 2 * D_FF), lay),
            _resident((None, FFN_CONV, 2 * D_FF), lay),
            _resident((None, 1, 2 * D_FF), lay),
            _resident((None, D_FF, D_MODEL), lay),
            vec,
        ],
        out_specs=pl.BlockSpec((TM_FFN, D_MODEL), row),
        out_shape=jax.ShapeDtypeStruct((TOKENS, D_MODEL), F32),
        scratch_shapes=[
            pltpu.VMEM((TM_FFN + SUBLANES, 2 * D_FF), F32),
            pltpu.VMEM((TM_FFN, D_FF), BF16),
        ],
        compiler_params=_params("arbitrary"),
        name="ffn",
    )(x, gpre, wup, cw, cb, wdown, gpost)


def kernel(x, mem, rel_bias, pre_mix_g, w_in, mconv_w, mconv_b, b_igate, b_fgate, attn_out_g,
           mlstm_out_g, w_out, post_mix_g, pre_mem_g, wq_mem, wk_mem, wv_mem, wo_mem, post_mem_g,
           pre_ffn_g, w_up, fconv_w, fconv_b, w_down, post_ffn_g):
    vec3 = lambda t: t.reshape(DEPTH, 1, -1)

    w_main = w_in[:, :, :MAIN_COLS].astype(BF16)
    n_gate = 2 * N_MLSTM_HEADS
    w_gate = jnp.pad(w_in[:, :, MAIN_COLS:], ((0, 0), (0, 0), (0, LANES - n_gate))).astype(BF16)
    gbias = jnp.pad(jnp.concatenate([b_igate, b_fgate], axis=1), ((0, 0), (0, LANES - n_gate)))
    ksc = jnp.concatenate([jnp.ones((1, MLSTM_WIDTH), F32),
                           jnp.full((1, MLSTM_WIDTH), MLSTM_HEAD_DIM ** -0.5, F32)], axis=1)
    w_out_bf = w_out.astype(BF16)
    wq_bf = wq_mem.astype(BF16)
    wo_bf = wo_mem.astype(BF16)
    wkv_bf = jnp.stack([wk_mem, wv_mem], axis=1).astype(BF16)
    w_up_bf = w_up.astype(BF16)
    w_down_bf = w_down.astype(BF16)
    bias = _attn_bias_table(rel_bias)

    kv = _memkv(mem.reshape(BATCH * N_MEM, D_MODEL).astype(BF16), wkv_bf)

    xs = x.reshape(TOKENS, D_MODEL)
    for l in range(DEPTH):
        qkv, qk, v, om, gates = _mix_in(l, xs, vec3(pre_mix_g), w_main, w_gate, mconv_w,
                                        vec3(mconv_b), ksc)
        attn = _dil_attn(qkv, bias)
        hmn = _mlstm(l, qk, v, om, gates, vec3(gbias), vec3(mlstm_out_g))
        xs = _mix_out(l, xs, attn, hmn, vec3(attn_out_g), w_out_bf, vec3(post_mix_g),
                      vec3(pre_mem_g), wq_bf, kv, wo_bf, vec3(post_mem_g))
        xs = _ffn(l, xs, vec3(pre_ffn_g), w_up_bf, fconv_w, vec3(fconv_b), w_down_bf,
                  vec3(post_ffn_g))
    return xs.reshape(BATCH, SEQ, D_MODEL)
```

```python
import functools
import math

import jax
import jax.numpy as jnp
import numpy as np
from jax import lax
from jax.experimental import pallas as pl
from jax.experimental.pallas import tpu as pltpu

F32 = jnp.float32
BF16 = jnp.bfloat16

D_MODEL = 1024
BATCH = 4
SEQ = 4096
DEPTH = 4
TOKENS = BATCH * SEQ

ATTN_HEAD_DIM = 64
ATTN_WIDTH = 512
N_ATTN_HEADS = 8
DILATIONS = (1, 4, 16)
ATTN_BLK = 128
N_MLSTM_HEADS = 4
MLSTM_WIDTH = 512
MLSTM_HEAD_DIM = 128
MLSTM_CHUNK = 128
MLSTM_CONV = 4
N_BUCKETS = 32
MAX_DISTANCE = 2048
N_MEM = 256
N_XHEADS = 4
XHEAD_DIM = 256
D_FF = 2816
FFN_CONV = 3
EPS = 1e-6

LANES = 128
SUBLANES = 8
NEG = -1e30
VMEM_LIMIT = 56 * 1024 * 1024

TM_IN = 512
SUB_IN = 256
MIXIN_COLS = 256
TM_OUT = 1024
SUB_OUT = 256
TM_FFN = 256
FFN_COLS = 256
MAIN_COLS = 3 * ATTN_WIDTH + 3 * MLSTM_WIDTH
GATE_ROWS = 2 * N_MLSTM_HEADS
MLSTM_BATCH = BATCH
MLSTM_OPERAND_ROWS = MLSTM_HEAD_DIM + 16

NT = (((1,), (1,)), ((), ()))
TN = (((0,), (0,)), ((), ()))


def _dot(a, b):
    return jnp.dot(a, b, preferred_element_type=F32)


def _dot_nt(a, b):
    return lax.dot_general(a, b, NT, preferred_element_type=F32)


def _rms(x, g):
    return x * lax.rsqrt(jnp.mean(x * x, axis=-1, keepdims=True) + EPS) * g


def _sigmoid(x):
    return 1.0 / (1.0 + jnp.exp(-x))


def _interleave(gens):
    pending = list(gens)
    active = []
    while pending or active:
        if pending:
            active.append(pending.pop(0))
        for g in list(active):
            try:
                next(g)
            except StopIteration:
                active.remove(g)


def _resident(block_shape, index_map):
    return pl.BlockSpec(block_shape, index_map, pipeline_mode=pl.Buffered(1))


def _params(*semantics):
    return pltpu.CompilerParams(dimension_semantics=semantics, vmem_limit_bytes=VMEM_LIMIT)


def _memkv_kernel(mem_ref, w_ref, out_ref):
    out_ref[...] = _dot(mem_ref[...], w_ref[...]).astype(BF16)


def _memkv(mem_bf, wkv_bf):
    rows = BATCH * N_MEM
    return pl.pallas_call(
        _memkv_kernel,
        grid=(DEPTH, 2),
        in_specs=[
            _resident((rows, D_MODEL), lambda l, j: (0, 0)),
            pl.BlockSpec((None, None, D_MODEL, D_MODEL), lambda l, j: (l, j, 0, 0)),
        ],
        out_specs=pl.BlockSpec((None, None, rows, D_MODEL), lambda l, j: (l, j, 0, 0)),
        out_shape=jax.ShapeDtypeStruct((DEPTH, 2, rows, D_MODEL), BF16),
        compiler_params=_params("parallel", "parallel"),
        name="memkv",
    )(mem_bf, wkv_bf)


def _mix_in_kernel(x_ref, g_ref, w_ref, wvt_ref, wg_ref, wgt_ref, cw_ref, cb_ref, ksc_ref,
                   qkv_ref, qk_ref, vt_ref, om_ref, gate_ref, gatet_ref, buf_ref):
    @pl.when(pl.program_id(0) % (SEQ // TM_IN) == 0)
    def _():
        buf_ref[0:SUBLANES, :] = jnp.zeros((SUBLANES, 2 * MLSTM_WIDTH), F32)

    a0 = 3 * ATTN_WIDTH
    a1 = a0 + 2 * MLSTM_WIDTH
    cw_cols = MIXIN_COLS
    n_conv = 2 * MLSTM_WIDTH // cw_cols

    def sub_tile(r0):
        rows = slice(r0, r0 + SUB_IN)
        h = _rms(x_ref[rows, :], g_ref[...]).astype(BF16)
        yield

        def plain(out_ref, w0, o0):
            out_ref[rows, o0:o0 + cw_cols] = _dot(h, w_ref[:, w0:w0 + cw_cols]).astype(BF16)

        def v_transposed(f0, _):
            vt_ref[f0:f0 + cw_cols, rows] = _dot_nt(wvt_ref[f0:f0 + cw_cols, :], h).astype(BF16)

        plain_jobs = ([(plain, qkv_ref, c, c) for c in range(0, a0, cw_cols)]
                      + [(v_transposed, c, None) for c in range(0, MLSTM_WIDTH, cw_cols)]
                      + [(plain, om_ref, a1 + c, c) for c in range(0, MLSTM_WIDTH, cw_cols)])
        per_conv = -(-len(plain_jobs) // n_conv)
        for cc in range(n_conv):
            cols = slice(cc * cw_cols, (cc + 1) * cw_cols)
            pre = _dot(h, w_ref[:, a0 + cc * cw_cols:a0 + (cc + 1) * cw_cols])
            buf_ref[SUBLANES + r0:SUBLANES + r0 + SUB_IN, cols] = pre
            yield
            for fn, *job in plain_jobs[cc * per_conv:(cc + 1) * per_conv]:
                fn(*job)
            y = cb_ref[:, cols]
            for j in range(MLSTM_CONV - 1):
                lo = SUBLANES + r0 - (MLSTM_CONV - 1) + j
                y = y + buf_ref[lo:lo + SUB_IN, cols] * cw_ref[j:j + 1, cols]
            y = y + pre * cw_ref[MLSTM_CONV - 1:MLSTM_CONV, cols]
            qk_ref[rows, cols] = (y * _sigmoid(y) * ksc_ref[:, cols]).astype(BF16)
            yield
        gate_ref[rows, :] = _dot(h, wg_ref[...])
        gatet_ref[:, rows] = _dot_nt(wgt_ref[...], h)[0:GATE_ROWS, :]

    _interleave([sub_tile(r0) for r0 in range(0, TM_IN, SUB_IN)])
    buf_ref[0:SUBLANES, :] = buf_ref[TM_IN:TM_IN + SUBLANES, :]


def _mix_in(l, x, pre_g, w_main, w_vt, w_gate, w_gate_t, cw, cb, ksc):
    row = lambda i: (i, 0)
    col = lambda i: (0, i)
    lay = lambda i: (l, 0, 0)
    return pl.pallas_call(
        _mix_in_kernel,
        grid=(TOKENS // TM_IN,),
        in_specs=[
            pl.BlockSpec((TM_IN, D_MODEL), row),
            _resident((None, 1, D_MODEL), lay),
            _resident((None, D_MODEL, MAIN_COLS), lay),
            _resident((None, MLSTM_WIDTH, D_MODEL), lay),
            _resident((None, D_MODEL, LANES), lay),
            _resident((None, 2 * GATE_ROWS, D_MODEL), lay),
            _resident((None, MLSTM_CONV, 2 * MLSTM_WIDTH), lay),
            _resident((None, 1, 2 * MLSTM_WIDTH), lay),
            _resident((1, 2 * MLSTM_WIDTH), lambda i: (0, 0)),
        ],
        out_specs=[
            pl.BlockSpec((TM_IN, 3 * ATTN_WIDTH), row),
            pl.BlockSpec((TM_IN, 2 * MLSTM_WIDTH), row),
            pl.BlockSpec((MLSTM_WIDTH, TM_IN), col),
            pl.BlockSpec((TM_IN, MLSTM_WIDTH), row),
            pl.BlockSpec((TM_IN, LANES), row),
            pl.BlockSpec((GATE_ROWS, TM_IN), col),
        ],
        out_shape=[
            jax.ShapeDtypeStruct((TOKENS, 3 * ATTN_WIDTH), BF16),
            jax.ShapeDtypeStruct((TOKENS, 2 * MLSTM_WIDTH), BF16),
            jax.ShapeDtypeStruct((MLSTM_WIDTH, TOKENS), BF16),
            jax.ShapeDtypeStruct((TOKENS, MLSTM_WIDTH), BF16),
            jax.ShapeDtypeStruct((TOKENS, LANES), F32),
            jax.ShapeDtypeStruct((GATE_ROWS, TOKENS), F32),
        ],
        scratch_shapes=[pltpu.VMEM((TM_IN + SUBLANES, 2 * MLSTM_WIDTH), F32)],
        compiler_params=_params("arbitrary"),
        name="mix_in",
    )(x, pre_g, w_main, w_vt, w_gate, w_gate_t, cw, cb, ksc)


N_UNITS = SEQ // ATTN_BLK
PIPE_DEPTH = 4
ONES_ROWS = 16


def _div(u, n):
    return lax.shift_right_logical(u, int(math.log2(n)))


def _mod(u, n):
    return lax.bitwise_and(u, n - 1)


def _dil_attn_kernel(q_ref, k_ref, v_ref, bias_ref, out_ref,
                     qf, kf, vf, q4f, k4f, v4f, dq, dk, dvt, o_s0, o_s1, o_s2, l_s0, l_s1, l_s2):
    d1, d4, d16 = DILATIONS
    hd = ATTN_HEAD_DIM
    dvt[LANES:LANES + ONES_ROWS, :] = jnp.ones((ONES_ROWS, SEQ), BF16)

    def widen(r, carry):
        rows = pl.ds(pl.multiple_of(r * ATTN_BLK, ATTN_BLK), ATTN_BLK)
        q = q_ref[rows, :].astype(F32) * (hd ** -0.5)
        k = k_ref[rows, :].astype(F32)
        v = v_ref[rows, :].astype(F32)
        qf[rows, :] = q
        kf[rows, :] = k
        vf[rows, :] = v
        dq[rows, :] = q.astype(BF16)
        dk[rows, :] = k.astype(BF16)
        dvt[0:LANES, rows] = v.T.astype(BF16)
        return carry

    lax.fori_loop(0, N_UNITS, widen, 0, unroll=4)

    lane = lax.broadcasted_iota(jnp.int32, (ATTN_BLK, LANES), 1)
    first_head = lane < hd

    def deinterleave(src_refs, keep_refs, stride, start_of):
        def body(u, carry):
            src = pl.ds(start_of(u), ATTN_BLK, stride=stride)
            dst = pl.ds(pl.multiple_of(u * ATTN_BLK, ATTN_BLK), ATTN_BLK)
            q, k, v = (r[src, :] for r in src_refs)
            if keep_refs is not None:
                for r, val in zip(keep_refs, (q, k, v)):
                    r[dst, :] = val
            dq[dst, :] = q.astype(BF16)
            dk[dst, :] = k.astype(BF16)
            dvt[0:LANES, dst] = v.T.astype(BF16)
            return carry
        lax.fori_loop(0, N_UNITS, body, 0, unroll=4)

    def run_pattern(p, d, o_scr, l_scr):
        nb = N_UNITS // d

        def qk(u):
            cur = pl.ds(u * ATTN_BLK, ATTN_BLK)
            prev = pl.ds(max(u - 1, 0) * ATTN_BLK, ATTN_BLK)
            q = dq[cur, :]
            zero = jnp.zeros_like(q)
            q2 = jnp.concatenate([jnp.where(first_head, q, zero), jnp.where(first_head, zero, q)], axis=0)
            kwin = jnp.concatenate([dk[prev, :], dk[cur, :]], axis=0)
            return _dot_nt(kwin, q2) + bias_ref[p, 0 if u % nb > 0 else 1]

        def rest(u, st):
            cur = pl.ds(u * ATTN_BLK, ATTN_BLK)
            prev = pl.ds(max(u - 1, 0) * ATTN_BLK, ATTN_BLK)
            m = jnp.max(st, axis=0, keepdims=True)
            e = jnp.exp(st - m)
            vtwin = jnp.concatenate([dvt[:, prev], dvt[:, cur]], axis=1)
            ot = _dot(vtwin, e.astype(BF16))
            s = ot[LANES:LANES + 1, :]
            inv = 1.0 / s
            lse = m + jnp.log(s)
            o_t = jnp.concatenate([ot[0:hd, 0:ATTN_BLK] * inv[:, 0:ATTN_BLK],
                                   ot[hd:2 * hd, ATTN_BLK:] * inv[:, ATTN_BLK:]], axis=0)
            l_t = jnp.concatenate([jnp.broadcast_to(lse[:, 0:ATTN_BLK], (hd, ATTN_BLK)),
                                   jnp.broadcast_to(lse[:, ATTN_BLK:], (hd, ATTN_BLK))], axis=0)
            dst = pl.ds(u // nb + (u % nb) * (ATTN_BLK * d), ATTN_BLK, stride=d)
            o_scr[dst, :] = o_t.T
            l_scr[dst, :] = l_t.T

        pending = {}
        for u in range(N_UNITS + PIPE_DEPTH):
            if u < N_UNITS:
                pending[u] = qk(u)
            if u >= PIPE_DEPTH:
                rest(u - PIPE_DEPTH, pending.pop(u - PIPE_DEPTH))

    run_pattern(0, d1, o_s0, l_s0)
    per4 = SEQ // d4
    nb4 = N_UNITS // d4
    deinterleave((qf, kf, vf), (q4f, k4f, v4f), d4,
                 lambda u: _div(u, nb4) + _mod(u, nb4) * (ATTN_BLK * d4))
    run_pattern(1, d4, o_s1, l_s1)
    ratio = d16 // d4
    nb16 = N_UNITS // d16
    deinterleave((q4f, k4f, v4f), None, ratio,
                 lambda u: (_mod(_div(u, nb16), d4) * per4 + _mod(u, nb16) * (ATTN_BLK * ratio)
                            + _div(_div(u, nb16), d4)))
    run_pattern(2, d16, o_s2, l_s2)

    def combine(r, carry):
        rows = pl.ds(pl.multiple_of(r * ATTN_BLK, ATTN_BLK), ATTN_BLK)
        la, lb, lc = l_s0[rows, :], l_s1[rows, :], l_s2[rows, :]
        mx = jnp.maximum(jnp.maximum(la, lb), lc)
        ea, eb, ec = jnp.exp(la - mx), jnp.exp(lb - mx), jnp.exp(lc - mx)
        mixed = (ea * o_s0[rows, :] + eb * o_s1[rows, :] + ec * o_s2[rows, :]) / (ea + eb + ec)
        out_ref[rows, :] = mixed.astype(BF16)
        return carry

    lax.fori_loop(0, N_UNITS, combine, 0, unroll=2)


def _dil_attn(qkv, bias):
    pairs = N_ATTN_HEADS // 2
    blk = (SEQ, LANES)
    f32_seq = pltpu.VMEM((SEQ, LANES), F32)
    bf_seq = pltpu.VMEM((SEQ, LANES), BF16)
    return pl.pallas_call(
        _dil_attn_kernel,
        grid=(BATCH, pairs),
        in_specs=[
            pl.BlockSpec(blk, lambda b, hp: (b, hp)),
            pl.BlockSpec(blk, lambda b, hp: (b, pairs + hp)),
            pl.BlockSpec(blk, lambda b, hp: (b, 2 * pairs + hp)),
            pl.BlockSpec((None, len(DILATIONS), 2, 2 * ATTN_BLK, 2 * ATTN_BLK),
                         lambda b, hp: (hp, 0, 0, 0, 0)),
        ],
        out_specs=pl.BlockSpec(blk, lambda b, hp: (b, hp)),
        out_shape=jax.ShapeDtypeStruct((TOKENS, ATTN_WIDTH), BF16),
        scratch_shapes=([f32_seq] * 6 + [bf_seq] * 2 + [pltpu.VMEM((LANES + ONES_ROWS, SEQ), BF16)]
                        + [f32_seq] * 6),
        compiler_params=_params("parallel", "parallel"),
        name="dil_attn",
    )(qkv, qkv, qkv, bias)


def _t5_causal_bucket(dist):
    max_exact = N_BUCKETS // 2
    d_f = jnp.maximum(dist, 1).astype(F32)
    large = max_exact + (jnp.log(d_f / max_exact) / math.log(MAX_DISTANCE / max_exact)
                         * (N_BUCKETS - max_exact)).astype(jnp.int32)
    large = jnp.minimum(large, N_BUCKETS - 1)
    return jnp.where(dist < max_exact, dist, large)


def _attn_bias_table(rel_bias):
    pairs = N_ATTN_HEADS // 2
    n_pat = len(DILATIONS)
    t = _attn_bias_qk(rel_bias).reshape(n_pat, pairs, 2, ATTN_BLK, 2 * ATTN_BLK)
    t = jnp.transpose(t, (1, 0, 4, 2, 3)).reshape(pairs, n_pat, 2 * ATTN_BLK, 2 * ATTN_BLK)
    key_in_prev = (jnp.arange(2 * ATTN_BLK) < ATTN_BLK)[None, None, :, None]
    return jnp.stack([t, jnp.where(key_in_prev, NEG, t)], axis=2)


def _attn_bias_qk(rel_bias):
    blk = ATTN_BLK
    period = 3 * blk + 1
    pad_lo = jnp.full((N_ATTN_HEADS, blk - 1), NEG, F32)
    pad_hi = jnp.full((N_ATTN_HEADS, period - 2 * blk), NEG, F32)
    tables = []
    for d in DILATIONS:
        bias_sub = rel_bias[_t5_causal_bucket(jnp.arange(blk + 1) * d)].T.astype(F32)
        r = jnp.concatenate([pad_lo, bias_sub[:, ::-1], pad_hi], axis=1)
        tiled = jnp.broadcast_to(r[:, None, :], (N_ATTN_HEADS, blk, period))
        skew = tiled.reshape(N_ATTN_HEADS, blk * period)[:, :blk * (period - 1)]
        skew = skew.reshape(N_ATTN_HEADS, blk, period - 1)
        tables.append(skew[:, :, blk - 1:3 * blk - 1])
    return jnp.stack(tables)


def _log_sigmoid(x):
    return jnp.minimum(x, 0.0) - jnp.log1p(jnp.exp(-jnp.abs(x)))


def _split2(x):
    hi = x.astype(BF16)
    return hi, (x - hi.astype(F32)).astype(BF16)


def _split3(x):
    hi, mid = _split2(x)
    lo = (x - hi.astype(F32) - mid.astype(F32)).astype(BF16)
    return hi, mid, lo


def _zip_phases(gens):
    pending = list(gens)
    active = []
    while pending or active:
        if pending:
            active.append(pending.pop(0))
        for g in list(active):
            try:
                next(g)
            except StopIteration:
                active.remove(g)
        yield


def _mlstm_kernel(*refs):
    nb = MLSTM_BATCH
    qk_ref, om_ref, gate_ref = refs[0:3]
    vt_refs = refs[3:3 + nb]
    gatet_refs = refs[3 + nb:3 + 2 * nb]
    gbias_ref, gbiast_ref, gout_ref, out_ref, c_scr, n_scr, m_scr = refs[3 + 2 * nb:]

    @pl.when(pl.program_id(1) == 0)
    def _():
        c_scr[...] = jnp.zeros_like(c_scr)
        n_scr[...] = jnp.zeros_like(n_scr)
        m_scr[...] = jnp.zeros_like(m_scr)

    nh = N_MLSTM_HEADS
    ch = MLSTM_CHUNK
    hd = MLSTM_HEAD_DIM
    row = lax.broadcasted_iota(jnp.int32, (ch, LANES), 0)
    lane = lax.broadcasted_iota(jnp.int32, (ch, LANES), 1)
    src_le_tgt = row <= lane
    tri_lower = (row >= lane).astype(BF16)
    tri_upper = src_le_tgt.astype(BF16)
    grow = lax.broadcasted_iota(jnp.int32, (GATE_ROWS, LANES), 0)
    pad_bf = jnp.zeros((MLSTM_OPERAND_ROWS - hd - 2, ch), BF16)

    def batch_row(bi):
        gates = gate_ref[bi] + gbias_ref[...]
        logf = jnp.where((lane >= nh) & (lane < 2 * nh), _log_sigmoid(gates), 0.0)
        hi, mid, lo = _split3(logf)
        bcum = _dot(tri_lower, hi) + _dot(tri_lower, mid) + _dot(tri_lower, lo)
        gates_t = gatet_refs[bi][...] + gbiast_ref[...]
        logf_t = jnp.where(grow >= nh, _log_sigmoid(gates_t), 0.0)
        hi, mid, lo = _split3(logf_t)
        bcum_t = _dot(hi, tri_upper) + _dot(mid, tri_upper) + _dot(lo, tri_upper)
        yield
        outs = [None] * nh

        def head(h):
            si = bi * nh + h
            hs = slice(h * hd, (h + 1) * hd)
            ks = slice(MLSTM_WIDTH + h * hd, MLSTM_WIDTH + (h + 1) * hd)
            q_h = qk_ref[bi, :, hs]
            k_h = qk_ref[bi, :, ks]
            vt_h = vt_refs[bi][hs, :]
            b_row = bcum_t[nh + h:nh + h + 1, :]
            i_row = gates_t[h:h + 1, :]
            w_col = gates[:, h:h + 1] - bcum[:, nh + h:nh + h + 1]
            g = b_row[:, ch - 1:ch]
            m_prev = m_scr[si:si + 1, :]
            s_t = _dot_nt(k_h, q_h)
            c_prev = c_scr[si]
            n_prev = n_scr[si:si + 1, :]
            carried = _dot_nt(jnp.concatenate([c_prev.astype(BF16), *_split2(n_prev), pad_bf], axis=0), q_h)
            yield
            d_t = jnp.where(src_le_tgt, b_row + w_col, NEG)
            a_row = b_row + m_prev
            m_row = jnp.maximum(a_row, jnp.max(d_t, axis=0, keepdims=True))
            p_t = jnp.exp(d_t - m_row) * s_t
            inter = jnp.exp(a_row - m_row)
            yield
            p_sum = jnp.sum(p_t, axis=0, keepdims=True)
            fresh = _dot(vt_h, p_t.astype(BF16))
            w_row = g - b_row + i_row
            m_new = jnp.maximum(g + m_prev, jnp.max(w_row, axis=1, keepdims=True))
            decay = jnp.exp(g + m_prev - m_new)
            wt = jnp.exp(w_row - m_new)
            upd = _dot(jnp.concatenate([(vt_h.astype(F32) * wt).astype(BF16), *_split2(wt), pad_bf], axis=0), k_h)
            c_scr[si] = decay[:, 0:1] * c_prev + upd[0:hd, :]
            n_scr[si:si + 1, :] = decay[:, 0:1] * n_prev + (upd[hd:hd + 1, :] + upd[hd + 1:hd + 2, :])
            m_scr[si:si + 1, :] = m_new
            yield
            num = inter * carried[0:hd, :] + fresh
            den = inter * (carried[hd:hd + 1, :] + carried[hd + 1:hd + 2, :]) + p_sum
            h_t = num * (1.0 / jnp.maximum(jnp.abs(den), jnp.exp(-m_row)))
            outs[h] = _sigmoid(om_ref[bi, :, hs].astype(F32)) * h_t.T

        for _ in _zip_phases([head(h) for h in range(nh)]):
            yield
        hm = jnp.concatenate(outs, axis=1)
        out_ref[bi] = _rms(hm, gout_ref[...]).astype(BF16)

    for _ in _zip_phases([batch_row(bi) for bi in range(nb)]):
        pass


def _mlstm(l, qk, vt, om, gates, gates_t, gbias, gbias_t, gout):
    nc = SEQ // MLSTM_CHUNK
    nb = MLSTM_BATCH
    blk3 = lambda g, c: (g, c, 0)
    lay = lambda g, c: (l, 0, 0)
    col = lambda bi: (lambda g, c: (0, (g * nb + bi) * nc + c))
    three = lambda t: t.reshape(BATCH, SEQ, t.shape[-1])
    n_chains = nb * N_MLSTM_HEADS
    out = pl.pallas_call(
        _mlstm_kernel,
        grid=(BATCH // nb, nc),
        in_specs=[
            pl.BlockSpec((nb, MLSTM_CHUNK, 2 * MLSTM_WIDTH), blk3),
            pl.BlockSpec((nb, MLSTM_CHUNK, MLSTM_WIDTH), blk3),
            pl.BlockSpec((nb, MLSTM_CHUNK, LANES), blk3),
        ] + [pl.BlockSpec((MLSTM_WIDTH, MLSTM_CHUNK), col(bi)) for bi in range(nb)]
          + [pl.BlockSpec((GATE_ROWS, MLSTM_CHUNK), col(bi)) for bi in range(nb)] + [
            _resident((None, 1, LANES), lay),
            _resident((None, GATE_ROWS, LANES), lay),
            _resident((None, 1, MLSTM_WIDTH), lay),
        ],
        out_specs=pl.BlockSpec((nb, MLSTM_CHUNK, MLSTM_WIDTH), blk3),
        out_shape=jax.ShapeDtypeStruct((BATCH, SEQ, MLSTM_WIDTH), BF16),
        scratch_shapes=[
            pltpu.VMEM((n_chains, MLSTM_HEAD_DIM, MLSTM_HEAD_DIM), F32),
            pltpu.VMEM((n_chains, MLSTM_HEAD_DIM), F32),
            pltpu.VMEM((n_chains, LANES), F32),
        ],
        compiler_params=_params("arbitrary", "arbitrary"),
        name="mlstm",
    )(three(qk), three(om), three(gates), *([vt] * nb), *([gates_t] * nb), gbias, gbias_t, gout)
    return out.reshape(TOKENS, MLSTM_WIDTH)


def _mix_out_kernel(x_ref, attn_ref, hmn_ref, gattn_ref, wout_ref, gpost_ref,
                    gpre_ref, wq_ref, k_ref, v_ref, wo_ref, gpostm_ref, out_ref):
    def sub_tile(r0):
        rows = slice(r0, r0 + SUB_OUT)
        an = _rms(attn_ref[rows, :].astype(F32), gattn_ref[...]).astype(BF16)
        yield
        h = _dot(an, wout_ref[0:ATTN_WIDTH, :]) + _dot(hmn_ref[rows, :], wout_ref[ATTN_WIDTH:, :])
        yield
        x1 = x_ref[rows, :] + _rms(h, gpost_ref[...])
        hq = _rms(x1, gpre_ref[...]).astype(BF16)
        yield
        q = (_dot(hq, wq_ref[...]) * (XHEAD_DIM ** -0.5)).astype(BF16)
        yield
        heads = []
        for hd in range(N_XHEADS):
            sl = slice(hd * XHEAD_DIM, (hd + 1) * XHEAD_DIM)
            logits = _dot_nt(q[:, sl], k_ref[:, sl])
            yield
            m = jnp.max(logits, axis=1, keepdims=True)
            e = jnp.exp(logits - m)
            s = jnp.sum(e, axis=1, keepdims=True)
            yield
            heads.append((_dot(e.astype(BF16), v_ref[:, sl]) / s).astype(BF16))
        o = jnp.concatenate(heads, axis=1)
        yield
        h2 = _dot(o, wo_ref[...])
        yield
        out_ref[rows, :] = x1 + _rms(h2, gpostm_ref[...])

    _interleave([sub_tile(r0) for r0 in range(0, TM_OUT, SUB_OUT)])


def _mix_out(l, x, attn, hmn, gattn, wout, gpost, gpre, wq, kv, wo, gpostm):
    row = lambda i: (i, 0)
    lay = lambda i: (l, 0, 0)
    tiles_per_seq = SEQ // TM_OUT
    vec = _resident((None, 1, D_MODEL), lay)
    mat = _resident((None, D_MODEL, D_MODEL), lay)
    return pl.pallas_call(
        _mix_out_kernel,
        grid=(TOKENS // TM_OUT,),
        in_specs=[
            pl.BlockSpec((TM_OUT, D_MODEL), row),
            pl.BlockSpec((TM_OUT, ATTN_WIDTH), row),
            pl.BlockSpec((TM_OUT, MLSTM_WIDTH), row),
            _resident((None, 1, ATTN_WIDTH), lay),
            mat, vec, vec, mat,
            pl.BlockSpec((None, None, N_MEM, D_MODEL), lambda i: (l, 0, i // tiles_per_seq, 0)),
            pl.BlockSpec((None, None, N_MEM, D_MODEL), lambda i: (l, 1, i // tiles_per_seq, 0)),
            mat, vec,
        ],
        out_specs=pl.BlockSpec((TM_OUT, D_MODEL), row),
        out_shape=jax.ShapeDtypeStruct((TOKENS, D_MODEL), F32),
        compiler_params=_params("parallel"),
        name="mix_out",
    )(x, attn, hmn, gattn, wout, gpost, gpre, wq, kv, kv, wo, gpostm)


def _ffn_kernel(x_ref, gpre_ref, wup_ref, cw_ref, cb_ref, wdown_ref, gpost_ref, out_ref,
                buf_ref, act_ref):
    @pl.when(pl.program_id(0) % (SEQ // TM_FFN) == 0)
    def _():
        buf_ref[0:SUBLANES, :] = jnp.zeros((SUBLANES, 2 * D_FF), F32)

    x = x_ref[...]
    h = _rms(x, gpre_ref[...]).astype(BF16)

    def conv_cols(c0):
        cols = slice(c0, c0 + FFN_COLS)
        u = _dot(h, wup_ref[:, cols])
        buf_ref[SUBLANES:SUBLANES + TM_FFN, cols] = u
        y = cb_ref[:, cols]
        for j in range(FFN_CONV - 1):
            lo = SUBLANES - (FFN_CONV - 1) + j
            y = y + buf_ref[lo:lo + TM_FFN, cols] * cw_ref[j:j + 1, cols]
        return y + u * cw_ref[FFN_CONV - 1:FFN_CONV, cols]

    for c in range(D_FF // FFN_COLS):
        a = conv_cols(c * FFN_COLS)
        g = conv_cols(D_FF + c * FFN_COLS)
        gelu = 0.5 * g * (1.0 + jnp.tanh(math.sqrt(2.0 / math.pi) * (g + 0.044715 * (g * g * g))))
        act_ref[:, c * FFN_COLS:(c + 1) * FFN_COLS] = (gelu * a).astype(BF16)
    buf_ref[0:SUBLANES, :] = buf_ref[TM_FFN:TM_FFN + SUBLANES, :]

    out_ref[...] = x + _rms(_dot(act_ref[...], wdown_ref[...]), gpost_ref[...])


def _ffn(l, x, gpre, wup, cw, cb, wdown, gpost):
    row = lambda i: (i, 0)
    lay = lambda i: (l, 0, 0)
    vec = _resident((None, 1, D_MODEL), lay)
    return pl.pallas_call(
        _ffn_kernel,
        grid=(TOKENS // TM_FFN,),
        in_specs=[
            pl.BlockSpec((TM_FFN, D_MODEL), row),
            vec,
            _resident((None, D_MODEL, 2 * D_FF), lay),
            _resident((None, FFN_CONV, 2 * D_FF), lay),
            _resident((None, 1, 2 * D_FF), lay),
            _resident((None, D_FF, D_MODEL), lay),
            vec,
        ],
        out_specs=pl.BlockSpec((TM_FFN, D_MODEL), row),
        out_shape=jax.ShapeDtypeStruct((TOKENS, D_MODEL), F32),
        scratch_shapes=[
            pltpu.VMEM((TM_FFN + SUBLANES, 2 * D_FF), F32),
            pltpu.VMEM((TM_FFN, D_FF), BF16),
        ],
        compiler_params=_params("arbitrary"),
        name="ffn",
    )(x, gpre, wup, cw, cb, wdown, gpost)


def kernel(x, mem, rel_bias, pre_mix_g, w_in, mconv_w, mconv_b, b_igate, b_fgate, attn_out_g,
           mlstm_out_g, w_out, post_mix_g, pre_mem_g, wq_mem, wk_mem, wv_mem, wo_mem, post_mem_g,
           pre_ffn_g, w_up, fconv_w, fconv_b, w_down, post_ffn_g):
    vec3 = lambda t: t.reshape(DEPTH, 1, -1)

    v0 = 3 * ATTN_WIDTH + 2 * MLSTM_WIDTH
    g0 = v0 + 2 * MLSTM_WIDTH
    w_main = jnp.concatenate([w_in[:, :, :v0], w_in[:, :, v0 + MLSTM_WIDTH:g0]], axis=2).astype(BF16)
    w_vt = jnp.swapaxes(w_in[:, :, v0:v0 + MLSTM_WIDTH], 1, 2).astype(BF16)
    w_gate = jnp.pad(w_in[:, :, g0:], ((0, 0), (0, 0), (0, LANES - GATE_ROWS))).astype(BF16)
    w_gate_t = jnp.pad(jnp.swapaxes(w_in[:, :, g0:], 1, 2), ((0, 0), (0, GATE_ROWS), (0, 0))).astype(BF16)
    gate_bias = jnp.concatenate([b_igate, b_fgate], axis=1)
    gbias = jnp.pad(gate_bias, ((0, 0), (0, LANES - GATE_ROWS)))
    gbias_t = jnp.broadcast_to(gate_bias[:, :, None], (DEPTH, GATE_ROWS, LANES))
    ksc = jnp.concatenate([jnp.ones((1, MLSTM_WIDTH), F32),
                           jnp.full((1, MLSTM_WIDTH), MLSTM_HEAD_DIM ** -0.5, F32)], axis=1)
    w_out_bf = w_out.astype(BF16)
    wq_bf = wq_mem.astype(BF16)
    wo_bf = wo_mem.astype(BF16)
    wkv_bf = jnp.stack([wk_mem, wv_mem], axis=1).astype(BF16)
    w_up_bf = w_up.astype(BF16)
    w_down_bf = w_down.astype(BF16)
    bias = _attn_bias_table(rel_bias)

    kv = _memkv(mem.reshape(BATCH * N_MEM, D_MODEL).astype(BF16), wkv_bf)

    xs = x.reshape(TOKENS, D_MODEL)
    for l in range(DEPTH):
        qkv, qk, vt, om, gates, gates_t = _mix_in(l, xs, vec3(pre_mix_g), w_main, w_vt, w_gate,
                                                  w_gate_t, mconv_w, vec3(mconv_b), ksc)
        attn = _dil_attn(qkv, bias)
        hmn = _mlstm(l, qk, vt, om, gates, gates_t, vec3(gbias), gbias_t, vec3(mlstm_out_g))
        xs = _mix_out(l, xs, attn, hmn, vec3(attn_out_g), w_out_bf, vec3(post_mix_g),
                      vec3(pre_mem_g), wq_bf, kv, wo_bf, vec3(post_mem_g))
        xs = _ffn(l, xs, vec3(pre_ffn_g), w_up_bf, fconv_w, vec3(fconv_b), w_down_bf,
                  vec3(post_ffn_g))
    return xs.reshape(BATCH, SEQ, D_MODEL)
```

```python
import functools
import math

import jax
import jax.numpy as jnp
import numpy as np
from jax import lax
from jax.experimental import pallas as pl
from jax.experimental.pallas import tpu as pltpu

F32 = jnp.float32
BF16 = jnp.bfloat16

D_MODEL = 1024
BATCH = 4
SEQ = 4096
DEPTH = 4
TOKENS = BATCH * SEQ

ATTN_HEAD_DIM = 64
ATTN_WIDTH = 512
N_ATTN_HEADS = 8
DILATIONS = (1, 4, 16)
ATTN_BLK = 128
N_MLSTM_HEADS = 4
MLSTM_WIDTH = 512
MLSTM_HEAD_DIM = 128
MLSTM_CHUNK = 128
MLSTM_CONV = 4
N_BUCKETS = 32
MAX_DISTANCE = 2048
N_MEM = 256
N_XHEADS = 4
XHEAD_DIM = 256
D_FF = 2816
FFN_CONV = 3
EPS = 1e-6

LANES = 128
SUBLANES = 8
NEG = -1e30
VMEM_LIMIT = 56 * 1024 * 1024

TM_IN = 1024
SUB_IN = 256
MIXIN_COLS = 256
TM_OUT = 1024
SUB_OUT = 256
TM_FFN = 512
SUB_FFN = 256
FFN_GROUP = SUBLANES * SUBLANES
FFN_COLS = 256
MAIN_COLS = 3 * ATTN_WIDTH + 3 * MLSTM_WIDTH
GATE_ROWS = 2 * N_MLSTM_HEADS
MLSTM_BATCH = BATCH
MLSTM_OPERAND_ROWS = MLSTM_HEAD_DIM + 16

NT = (((1,), (1,)), ((), ()))
TN = (((0,), (0,)), ((), ()))


def _dot(a, b):
    return jnp.dot(a, b, preferred_element_type=F32)


def _dot_nt(a, b):
    return lax.dot_general(a, b, NT, preferred_element_type=F32)


def _rms(x, g):
    return x * lax.rsqrt(jnp.mean(x * x, axis=-1, keepdims=True) + EPS) * g


def _sigmoid(x):
    return 1.0 / (1.0 + jnp.exp(-x))


def _interleave(gens):
    pending = list(gens)
    active = []
    while pending or active:
        if pending:
            active.append(pending.pop(0))
        for g in list(active):
            try:
                next(g)
            except StopIteration:
                active.remove(g)


def _resident(block_shape, index_map):
    return pl.BlockSpec(block_shape, index_map, pipeline_mode=pl.Buffered(1))


def _params(*semantics):
    return pltpu.CompilerParams(dimension_semantics=semantics, vmem_limit_bytes=VMEM_LIMIT)


def _memkv_kernel(mem_ref, wk_ref, wv_ref, k_ref, v_ref):
    mem = mem_ref[...]
    k_ref[...] = _dot(mem, wk_ref[...].astype(BF16)).astype(BF16)
    v_ref[...] = _dot(mem, wv_ref[...].astype(BF16)).astype(BF16)


def _memkv(mem_bf, wk, wv):
    rows = BATCH * N_MEM
    w_spec = pl.BlockSpec((None, D_MODEL, D_MODEL), lambda l: (l, 0, 0))
    o_spec = pl.BlockSpec((None, rows, D_MODEL), lambda l: (l, 0, 0))
    o_shape = jax.ShapeDtypeStruct((DEPTH, rows, D_MODEL), BF16)
    return pl.pallas_call(
        _memkv_kernel,
        grid=(DEPTH,),
        in_specs=[_resident((rows, D_MODEL), lambda l: (0, 0)), w_spec, w_spec],
        out_specs=[o_spec, o_spec],
        out_shape=[o_shape, o_shape],
        compiler_params=_params("parallel"),
        name="memkv",
    )(mem_bf, wk, wv)


def _mix_in_kernel(x_ref, g_ref, w_ref, wvt_ref, wg_ref, wgt_ref, cw_ref, cb_ref, ksc_ref,
                   qkv_ref, qk_ref, vt_ref, om_ref, gate_ref, gatet_ref, buf_ref):
    @pl.when(pl.program_id(0) % (SEQ // TM_IN) == 0)
    def _():
        buf_ref[0:SUBLANES, :] = jnp.zeros((SUBLANES, 2 * MLSTM_WIDTH), F32)

    a0 = 3 * ATTN_WIDTH
    a1 = a0 + 2 * MLSTM_WIDTH
    cw_cols = MIXIN_COLS
    n_conv = 2 * MLSTM_WIDTH // cw_cols

    def sub_tile(r0):
        rows = slice(r0, r0 + SUB_IN)
        h = _rms(x_ref[rows, :], g_ref[...]).astype(BF16)
        yield

        def plain(out_ref, w0, o0):
            out_ref[rows, o0:o0 + cw_cols] = _dot(h, w_ref[:, w0:w0 + cw_cols]).astype(BF16)

        def v_transposed(f0, _):
            vt_ref[f0:f0 + cw_cols, rows] = _dot_nt(wvt_ref[f0:f0 + cw_cols, :], h).astype(BF16)

        plain_jobs = ([(plain, qkv_ref, c, c) for c in range(0, a0, cw_cols)]
                      + [(v_transposed, c, None) for c in range(0, MLSTM_WIDTH, cw_cols)]
                      + [(plain, om_ref, a1 + c, c) for c in range(0, MLSTM_WIDTH, cw_cols)])
        per_conv = -(-len(plain_jobs) // n_conv)
        for cc in range(n_conv):
            cols = slice(cc * cw_cols, (cc + 1) * cw_cols)
            pre = _dot(h, w_ref[:, a0 + cc * cw_cols:a0 + (cc + 1) * cw_cols])
            buf_ref[SUBLANES + r0:SUBLANES + r0 + SUB_IN, cols] = pre
            yield
            for fn, *job in plain_jobs[cc * per_conv:(cc + 1) * per_conv]:
                fn(*job)
            y = cb_ref[:, cols]
            for j in range(MLSTM_CONV - 1):
                lo = SUBLANES + r0 - (MLSTM_CONV - 1) + j
                y = y + buf_ref[lo:lo + SUB_IN, cols] * cw_ref[j:j + 1, cols]
            y = y + pre * cw_ref[MLSTM_CONV - 1:MLSTM_CONV, cols]
            qk_ref[rows, cols] = (y * _sigmoid(y) * ksc_ref[:, cols]).astype(BF16)
            yield
        gate_ref[rows, :] = _dot(h, wg_ref[...])
        gatet_ref[:, rows] = _dot_nt(wgt_ref[...], h)[0:GATE_ROWS, :]

    _interleave([sub_tile(r0) for r0 in range(0, TM_IN, SUB_IN)])
    buf_ref[0:SUBLANES, :] = buf_ref[TM_IN:TM_IN + SUBLANES, :]


def _mix_in(l, x, pre_g, w_main, w_vt, w_gate, w_gate_t, cw, cb, ksc):
    row = lambda i: (i, 0)
    col = lambda i: (0, i)
    lay = lambda i: (l, 0, 0)
    return pl.pallas_call(
        _mix_in_kernel,
        grid=(TOKENS // TM_IN,),
        in_specs=[
            pl.BlockSpec((TM_IN, D_MODEL), row),
            _resident((None, 1, D_MODEL), lay),
            _resident((None, D_MODEL, MAIN_COLS), lay),
            _resident((None, MLSTM_WIDTH, D_MODEL), lay),
            _resident((None, D_MODEL, LANES), lay),
            _resident((None, 2 * GATE_ROWS, D_MODEL), lay),
            _resident((None, MLSTM_CONV, 2 * MLSTM_WIDTH), lay),
            _resident((None, 1, 2 * MLSTM_WIDTH), lay),
            _resident((1, 2 * MLSTM_WIDTH), lambda i: (0, 0)),
        ],
        out_specs=[
            pl.BlockSpec((TM_IN, 3 * ATTN_WIDTH), row),
            pl.BlockSpec((TM_IN, 2 * MLSTM_WIDTH), row),
            pl.BlockSpec((MLSTM_WIDTH, TM_IN), col),
            pl.BlockSpec((TM_IN, MLSTM_WIDTH), row),
            pl.BlockSpec((TM_IN, LANES), row),
            pl.BlockSpec((GATE_ROWS, TM_IN), col),
        ],
        out_shape=[
            jax.ShapeDtypeStruct((TOKENS, 3 * ATTN_WIDTH), BF16),
            jax.ShapeDtypeStruct((TOKENS, 2 * MLSTM_WIDTH), BF16),
            jax.ShapeDtypeStruct((MLSTM_WIDTH, TOKENS), BF16),
            jax.ShapeDtypeStruct((TOKENS, MLSTM_WIDTH), BF16),
            jax.ShapeDtypeStruct((TOKENS, LANES), F32),
            jax.ShapeDtypeStruct((GATE_ROWS, TOKENS), F32),
        ],
        scratch_shapes=[pltpu.VMEM((TM_IN + SUBLANES, 2 * MLSTM_WIDTH), F32)],
        compiler_params=_params("arbitrary"),
        name="mix_in",
    )(x, pre_g, w_main, w_vt, w_gate, w_gate_t, cw, cb, ksc)


N_UNITS = SEQ // ATTN_BLK
PIPE_DEPTH = 4
ONES_ROWS = 16


def _div(u, n):
    return lax.shift_right_logical(u, int(math.log2(n)))


def _mod(u, n):
    return lax.bitwise_and(u, n - 1)


def _dil_attn_kernel(q_ref, k_ref, v_ref, bias_ref, out_ref,
                     qf, kf, vf, q4f, k4f, v4f, dq, dk, dvt, o_s0, o_s1, o_s2, l_s0, l_s1, l_s2):
    d1, d4, d16 = DILATIONS
    hd = ATTN_HEAD_DIM
    dvt[LANES:LANES + ONES_ROWS, :] = jnp.ones((ONES_ROWS, SEQ), BF16)

    def widen(r, carry):
        rows = pl.ds(pl.multiple_of(r * ATTN_BLK, ATTN_BLK), ATTN_BLK)
        q = q_ref[rows, :].astype(F32) * (hd ** -0.5)
        k = k_ref[rows, :].astype(F32)
        v = v_ref[rows, :].astype(F32)
        qf[rows, :] = q
        kf[rows, :] = k
        vf[rows, :] = v
        dq[rows, :] = q.astype(BF16)
        dk[rows, :] = k.astype(BF16)
        dvt[0:LANES, rows] = v.T.astype(BF16)
        return carry

    lax.fori_loop(0, N_UNITS, widen, 0, unroll=4)

    lane = lax.broadcasted_iota(jnp.int32, (ATTN_BLK, LANES), 1)
    first_head = lane < hd
    key_in_prev = lax.broadcasted_iota(jnp.int32, (2 * ATTN_BLK, 2 * ATTN_BLK), 0) < ATTN_BLK

    def deinterleave(src_refs, keep_refs, stride, start_of):
        def body(u, carry):
            src = pl.ds(start_of(u), ATTN_BLK, stride=stride)
            dst = pl.ds(pl.multiple_of(u * ATTN_BLK, ATTN_BLK), ATTN_BLK)
            q, k, v = (r[src, :] for r in src_refs)
            if keep_refs is not None:
                for r, val in zip(keep_refs, (q, k, v)):
                    r[dst, :] = val
            dq[dst, :] = q.astype(BF16)
            dk[dst, :] = k.astype(BF16)
            dvt[0:LANES, dst] = v.T.astype(BF16)
            return carry
        lax.fori_loop(0, N_UNITS, body, 0, unroll=4)

    def run_pattern(p, d, o_scr, l_scr):
        nb = N_UNITS // d

        def qk(u):
            cur = pl.ds(u * ATTN_BLK, ATTN_BLK)
            prev = pl.ds(max(u - 1, 0) * ATTN_BLK, ATTN_BLK)
            q = dq[cur, :]
            zero = jnp.zeros_like(q)
            q2 = jnp.concatenate([jnp.where(first_head, q, zero), jnp.where(first_head, zero, q)], axis=0)
            kwin = jnp.concatenate([dk[prev, :], dk[cur, :]], axis=0)
            bias = bias_ref[p]
            if u % nb == 0:
                bias = jnp.where(key_in_prev, NEG, bias)
            return _dot_nt(kwin, q2) + bias

        def rest(u, st):
            cur = pl.ds(u * ATTN_BLK, ATTN_BLK)
            prev = pl.ds(max(u - 1, 0) * ATTN_BLK, ATTN_BLK)
            m = jnp.max(st, axis=0, keepdims=True)
            e = jnp.exp(st - m)
            vtwin = jnp.concatenate([dvt[:, prev], dvt[:, cur]], axis=1)
            ot = _dot(vtwin, e.astype(BF16))
            s = ot[LANES:LANES + 1, :]
            inv = 1.0 / s
            lse = m + jnp.log(s)
            o_t = jnp.concatenate([ot[0:hd, 0:ATTN_BLK] * inv[:, 0:ATTN_BLK],
                                   ot[hd:2 * hd, ATTN_BLK:] * inv[:, ATTN_BLK:]], axis=0)
            l_t = jnp.concatenate([jnp.broadcast_to(lse[:, 0:ATTN_BLK], (hd, ATTN_BLK)),
                                   jnp.broadcast_to(lse[:, ATTN_BLK:], (hd, ATTN_BLK))], axis=0)
            dst = pl.ds(u // nb + (u % nb) * (ATTN_BLK * d), ATTN_BLK, stride=d)
            o_scr[dst, :] = o_t.T
            l_scr[dst, :] = l_t.T

        pending = {}
        for u in range(N_UNITS + PIPE_DEPTH):
            if u < N_UNITS:
                pending[u] = qk(u)
            if u >= PIPE_DEPTH:
                rest(u - PIPE_DEPTH, pending.pop(u - PIPE_DEPTH))

    run_pattern(0, d1, o_s0, l_s0)
    per4 = SEQ // d4
    nb4 = N_UNITS // d4
    deinterleave((qf, kf, vf), (q4f, k4f, v4f), d4,
                 lambda u: _div(u, nb4) + _mod(u, nb4) * (ATTN_BLK * d4))
    run_pattern(1, d4, o_s1, l_s1)
    ratio = d16 // d4
    nb16 = N_UNITS // d16
    deinterleave((q4f, k4f, v4f), None, ratio,
                 lambda u: (_mod(_div(u, nb16), d4) * per4 + _mod(u, nb16) * (ATTN_BLK * ratio)
                            + _div(_div(u, nb16), d4)))
    run_pattern(2, d16, o_s2, l_s2)

    def combine(r, carry):
        rows = pl.ds(pl.multiple_of(r * ATTN_BLK, ATTN_BLK), ATTN_BLK)
        la, lb, lc = l_s0[rows, :], l_s1[rows, :], l_s2[rows, :]
        mx = jnp.maximum(jnp.maximum(la, lb), lc)
        ea, eb, ec = jnp.exp(la - mx), jnp.exp(lb - mx), jnp.exp(lc - mx)
        mixed = (ea * o_s0[rows, :] + eb * o_s1[rows, :] + ec * o_s2[rows, :]) / (ea + eb + ec)
        out_ref[rows, :] = mixed.astype(BF16)
        return carry

    lax.fori_loop(0, N_UNITS, combine, 0, unroll=2)


def _dil_attn(qkv, bias):
    pairs = N_ATTN_HEADS // 2
    blk = (SEQ, LANES)
    f32_seq = pltpu.VMEM((SEQ, LANES), F32)
    bf_seq = pltpu.VMEM((SEQ, LANES), BF16)
    return pl.pallas_call(
        _dil_attn_kernel,
        grid=(BATCH, pairs),
        in_specs=[
            pl.BlockSpec(blk, lambda b, hp: (b, hp)),
            pl.BlockSpec(blk, lambda b, hp: (b, pairs + hp)),
            pl.BlockSpec(blk, lambda b, hp: (b, 2 * pairs + hp)),
            pl.BlockSpec((None, len(DILATIONS), 2 * ATTN_BLK, 2 * ATTN_BLK),
                         lambda b, hp: (hp, 0, 0, 0)),
        ],
        out_specs=pl.BlockSpec(blk, lambda b, hp: (b, hp)),
        out_shape=jax.ShapeDtypeStruct((TOKENS, ATTN_WIDTH), BF16),
        scratch_shapes=([f32_seq] * 6 + [bf_seq] * 2 + [pltpu.VMEM((LANES + ONES_ROWS, SEQ), BF16)]
                        + [f32_seq] * 6),
        compiler_params=_params("parallel", "parallel"),
        name="dil_attn",
    )(qkv, qkv, qkv, bias)


def _t5_causal_bucket(dist):
    max_exact = N_BUCKETS // 2
    d_f = jnp.maximum(dist, 1).astype(F32)
    large = max_exact + (jnp.log(d_f / max_exact) / math.log(MAX_DISTANCE / max_exact)
                         * (N_BUCKETS - max_exact)).astype(jnp.int32)
    large = jnp.minimum(large, N_BUCKETS - 1)
    return jnp.where(dist < max_exact, dist, large)


def _attn_bias_table(rel_bias):
    pairs = N_ATTN_HEADS // 2
    n_pat = len(DILATIONS)
    t = _attn_bias_qk(rel_bias).reshape(n_pat, pairs, 2, ATTN_BLK, 2 * ATTN_BLK)
    return jnp.transpose(t, (1, 0, 4, 2, 3)).reshape(pairs, n_pat, 2 * ATTN_BLK, 2 * ATTN_BLK)


def _attn_bias_qk(rel_bias):
    blk = ATTN_BLK
    period = 3 * blk + 1
    pad_lo = jnp.full((N_ATTN_HEADS, blk - 1), NEG, F32)
    pad_hi = jnp.full((N_ATTN_HEADS, period - 2 * blk), NEG, F32)
    tables = []
    for d in DILATIONS:
        bias_sub = rel_bias[_t5_causal_bucket(jnp.arange(blk + 1) * d)].T.astype(F32)
        r = jnp.concatenate([pad_lo, bias_sub[:, ::-1], pad_hi], axis=1)
        tiled = jnp.broadcast_to(r[:, None, :], (N_ATTN_HEADS, blk, period))
        skew = tiled.reshape(N_ATTN_HEADS, blk * period)[:, :blk * (period - 1)]
        skew = skew.reshape(N_ATTN_HEADS, blk, period - 1)
        tables.append(skew[:, :, blk - 1:3 * blk - 1])
    return jnp.stack(tables)


def _log_sigmoid(x):
    return jnp.minimum(x, 0.0) - jnp.log1p(jnp.exp(-jnp.abs(x)))


def _split2(x):
    hi = x.astype(BF16)
    return hi, (x - hi.astype(F32)).astype(BF16)


def _split3(x):
    hi, mid = _split2(x)
    lo = (x - hi.astype(F32) - mid.astype(F32)).astype(BF16)
    return hi, mid, lo


def _zip_phases(gens):
    pending = list(gens)
    active = []
    while pending or active:
        if pending:
            active.append(pending.pop(0))
        for g in list(active):
            try:
                next(g)
            except StopIteration:
                active.remove(g)
        yield


def _mlstm_kernel(*refs):
    nb = MLSTM_BATCH
    qk_ref, om_ref, gate_ref = refs[0:3]
    vt_refs = refs[3:3 + nb]
    gatet_refs = refs[3 + nb:3 + 2 * nb]
    gbias_ref, gbiast_ref, gout_ref, out_ref, c_scr, n_scr, m_scr = refs[3 + 2 * nb:]

    @pl.when(pl.program_id(1) == 0)
    def _():
        c_scr[...] = jnp.zeros_like(c_scr)
        n_scr[...] = jnp.zeros_like(n_scr)
        m_scr[...] = jnp.zeros_like(m_scr)

    nh = N_MLSTM_HEADS
    ch = MLSTM_CHUNK
    hd = MLSTM_HEAD_DIM
    row = lax.broadcasted_iota(jnp.int32, (ch, LANES), 0)
    lane = lax.broadcasted_iota(jnp.int32, (ch, LANES), 1)
    src_le_tgt = row <= lane
    tri_lower = (row >= lane).astype(BF16)
    tri_upper = src_le_tgt.astype(BF16)
    grow = lax.broadcasted_iota(jnp.int32, (GATE_ROWS, LANES), 0)
    pad_bf = jnp.zeros((MLSTM_OPERAND_ROWS - hd - 2, ch), BF16)

    def batch_row(bi):
        gates = gate_ref[bi] + gbias_ref[...]
        logf = jnp.where((lane >= nh) & (lane < 2 * nh), _log_sigmoid(gates), 0.0)
        hi, mid, lo = _split3(logf)
        bcum = _dot(tri_lower, hi) + _dot(tri_lower, mid) + _dot(tri_lower, lo)
        gates_t = gatet_refs[bi][...] + gbiast_ref[...]
        logf_t = jnp.where(grow >= nh, _log_sigmoid(gates_t), 0.0)
        hi, mid, lo = _split3(logf_t)
        bcum_t = _dot(hi, tri_upper) + _dot(mid, tri_upper) + _dot(lo, tri_upper)
        yield
        outs = [None] * nh

        def head(h):
            si = bi * nh + h
            hs = slice(h * hd, (h + 1) * hd)
            ks = slice(MLSTM_WIDTH + h * hd, MLSTM_WIDTH + (h + 1) * hd)
            q_h = qk_ref[bi, :, hs]
            k_h = qk_ref[bi, :, ks]
            vt_h = vt_refs[bi][hs, :]
            b_row = bcum_t[nh + h:nh + h + 1, :]
            i_row = gates_t[h:h + 1, :]
            w_col = gates[:, h:h + 1] - bcum[:, nh + h:nh + h + 1]
            g = b_row[:, ch - 1:ch]
            m_prev = m_scr[si:si + 1, :]
            s_t = _dot_nt(k_h, q_h)
            c_prev = c_scr[si]
            n_prev = n_scr[si:si + 1, :]
            carried = _dot_nt(jnp.concatenate([c_prev.astype(BF16), *_split2(n_prev), pad_bf], axis=0), q_h)
            yield
            d_t = jnp.where(src_le_tgt, b_row + w_col, NEG)
            a_row = b_row + m_prev
            m_row = jnp.maximum(a_row, jnp.max(d_t, axis=0, keepdims=True))
            p_t = jnp.exp(d_t - m_row) * s_t
            inter = jnp.exp(a_row - m_row)
            yield
            p_sum = jnp.sum(p_t, axis=0, keepdims=True)
            fresh = _dot(vt_h, p_t.astype(BF16))
            w_row = g - b_row + i_row
            m_new = jnp.maximum(g + m_prev, jnp.max(w_row, axis=1, keepdims=True))
            decay = jnp.exp(g + m_prev - m_new)
            wt = jnp.exp(w_row - m_new)
            upd = _dot(jnp.concatenate([(vt_h.astype(F32) * wt).astype(BF16), *_split2(wt), pad_bf], axis=0), k_h)
            c_scr[si] = decay[:, 0:1] * c_prev + upd[0:hd, :]
            n_scr[si:si + 1, :] = decay[:, 0:1] * n_prev + (upd[hd:hd + 1, :] + upd[hd + 1:hd + 2, :])
            m_scr[si:si + 1, :] = m_new
            yield
            num = inter * carried[0:hd, :] + fresh
            den = inter * (carried[hd:hd + 1, :] + carried[hd + 1:hd + 2, :]) + p_sum
            h_t = num * (1.0 / jnp.maximum(jnp.abs(den), jnp.exp(-m_row)))
            outs[h] = _sigmoid(om_ref[bi, :, hs].astype(F32)) * h_t.T

        for _ in _zip_phases([head(h) for h in range(nh)]):
            yield
        hm = jnp.concatenate(outs, axis=1)
        out_ref[bi] = _rms(hm, gout_ref[...]).astype(BF16)

    for _ in _zip_phases([batch_row(bi) for bi in range(nb)]):
        pass


def _mlstm(l, qk, vt, om, gates, gates_t, gbias, gbias_t, gout):
    nc = SEQ // MLSTM_CHUNK
    nb = MLSTM_BATCH
    blk3 = lambda g, c: (g, c, 0)
    lay = lambda g, c: (l, 0, 0)
    col = lambda bi: (lambda g, c: (0, (g * nb + bi) * nc + c))
    three = lambda t: t.reshape(BATCH, SEQ, t.shape[-1])
    n_chains = nb * N_MLSTM_HEADS
    out = pl.pallas_call(
        _mlstm_kernel,
        grid=(BATCH // nb, nc),
        in_specs=[
            pl.BlockSpec((nb, MLSTM_CHUNK, 2 * MLSTM_WIDTH), blk3),
            pl.BlockSpec((nb, MLSTM_CHUNK, MLSTM_WIDTH), blk3),
            pl.BlockSpec((nb, MLSTM_CHUNK, LANES), blk3),
        ] + [pl.BlockSpec((MLSTM_WIDTH, MLSTM_CHUNK), col(bi)) for bi in range(nb)]
          + [pl.BlockSpec((GATE_ROWS, MLSTM_CHUNK), col(bi)) for bi in range(nb)] + [
            _resident((None, 1, LANES), lay),
            _resident((None, GATE_ROWS, LANES), lay),
            _resident((None, 1, MLSTM_WIDTH), lay),
        ],
        out_specs=pl.BlockSpec((nb, MLSTM_CHUNK, MLSTM_WIDTH), blk3),
        out_shape=jax.ShapeDtypeStruct((BATCH, SEQ, MLSTM_WIDTH), BF16),
        scratch_shapes=[
            pltpu.VMEM((n_chains, MLSTM_HEAD_DIM, MLSTM_HEAD_DIM), F32),
            pltpu.VMEM((n_chains, MLSTM_HEAD_DIM), F32),
            pltpu.VMEM((n_chains, LANES), F32),
        ],
        compiler_params=_params("arbitrary", "arbitrary"),
        name="mlstm",
    )(three(qk), three(om), three(gates), *([vt] * nb), *([gates_t] * nb), gbias, gbias_t, gout)
    return out.reshape(TOKENS, MLSTM_WIDTH)


def _mix_out_kernel(x_ref, attn_ref, hmn_ref, gattn_ref, wout_ref, gpost_ref,
                    gpre_ref, wq_ref, k_ref, v_ref, wo_ref, gpostm_ref, out_ref):
    def sub_tile(r0):
        rows = slice(r0, r0 + SUB_OUT)
        an = _rms(attn_ref[rows, :].astype(F32), gattn_ref[...]).astype(BF16)
        yield
        h = _dot(an, wout_ref[0:ATTN_WIDTH, :]) + _dot(hmn_ref[rows, :], wout_ref[ATTN_WIDTH:, :])
        yield
        x1 = x_ref[rows, :] + _rms(h, gpost_ref[...])
        hq = _rms(x1, gpre_ref[...]).astype(BF16)
        yield
        q = (_dot(hq, wq_ref[...]) * (XHEAD_DIM ** -0.5)).astype(BF16)
        yield
        heads = []
        for hd in range(N_XHEADS):
            sl = slice(hd * XHEAD_DIM, (hd + 1) * XHEAD_DIM)
            logits = _dot_nt(q[:, sl], k_ref[:, sl])
            yield
            m = jnp.max(logits, axis=1, keepdims=True)
            e = jnp.exp(logits - m)
            s = jnp.sum(e, axis=1, keepdims=True)
            yield
            heads.append((_dot(e.astype(BF16), v_ref[:, sl]) / s).astype(BF16))
        o = jnp.concatenate(heads, axis=1)
        yield
        h2 = _dot(o, wo_ref[...])
        yield
        out_ref[rows, :] = x1 + _rms(h2, gpostm_ref[...])

    _interleave([sub_tile(r0) for r0 in range(0, TM_OUT, SUB_OUT)])


def _mix_out(l, x, attn, hmn, gattn, wout, gpost, gpre, wq, k_mem, v_mem, wo, gpostm):
    row = lambda i: (i, 0)
    lay = lambda i: (l, 0, 0)
    tiles_per_seq = SEQ // TM_OUT
    vec = _resident((None, 1, D_MODEL), lay)
    mat = _resident((None, D_MODEL, D_MODEL), lay)
    return pl.pallas_call(
        _mix_out_kernel,
        grid=(TOKENS // TM_OUT,),
        in_specs=[
            pl.BlockSpec((TM_OUT, D_MODEL), row),
            pl.BlockSpec((TM_OUT, ATTN_WIDTH), row),
            pl.BlockSpec((TM_OUT, MLSTM_WIDTH), row),
            _resident((None, 1, ATTN_WIDTH), lay),
            mat, vec, vec, mat,
            pl.BlockSpec((None, N_MEM, D_MODEL), lambda i: (l, i // tiles_per_seq, 0)),
            pl.BlockSpec((None, N_MEM, D_MODEL), lambda i: (l, i // tiles_per_seq, 0)),
            mat, vec,
        ],
        out_specs=pl.BlockSpec((TM_OUT, D_MODEL), row),
        out_shape=jax.ShapeDtypeStruct((TOKENS, D_MODEL), F32),
        compiler_params=_params("parallel"),
        name="mix_out",
    )(x, attn, hmn, gattn, wout, gpost, gpre, wq, k_mem, v_mem, wo, gpostm)


def _ffn_kernel(x_ref, gpre_ref, wup_ref, cw_ref, cb_ref, wdown_ref, gpost_ref, out_ref,
                carry_ref, act_ref, xs_ref, ys_ref):
    assert FFN_CONV == 3

    @pl.when(pl.program_id(0) % (SEQ // TM_FFN) == 0)
    def _():
        carry_ref[...] = jnp.zeros_like(carry_ref)

    n_groups = SUB_FFN // FFN_GROUP
    n_slabs = D_MODEL // LANES
    last_sub = lax.broadcasted_iota(jnp.int32, (SUBLANES, FFN_COLS), 0) == SUBLANES - 1

    for slab, r0 in enumerate(range(0, TM_FFN, SUB_FFN)):
        rows = slice(r0, r0 + SUB_FFN)
        for j in range(n_slabs):
            xs_ref[slab, j] = x_ref[rows, j * LANES:(j + 1) * LANES]
        x = jnp.concatenate([
            jnp.concatenate([xs_ref.at[slab, j][pl.ds(g * FFN_GROUP + v, SUBLANES, stride=SUBLANES), :]
                             for g in range(n_groups) for v in range(SUBLANES)], axis=0)
            for j in range(n_slabs)], axis=1)
        h = _rms(x, gpre_ref[...]).astype(BF16)

        def conv_cols(c0):
            cols = slice(c0, c0 + FFN_COLS)
            u = _dot(h, wup_ref[:, cols])
            vregs = [[u[g * FFN_GROUP + v * SUBLANES:g * FFN_GROUP + (v + 1) * SUBLANES, :]
                      for v in range(SUBLANES)] for g in range(n_groups)]
            prev6 = carry_ref[0:SUBLANES, cols]
            prev7 = carry_ref[SUBLANES:2 * SUBLANES, cols]
            back1, back2 = [], []
            for g in range(n_groups):
                a6 = pltpu.roll(jnp.where(last_sub, prev6, vregs[g][6]), 1, axis=0)
                a7 = pltpu.roll(jnp.where(last_sub, prev7, vregs[g][7]), 1, axis=0)
                back1 += [a7] + vregs[g][0:7]
                back2 += [a6, a7] + vregs[g][0:6]
                prev6, prev7 = vregs[g][6], vregs[g][7]
            carry_ref[0:SUBLANES, cols] = prev6
            carry_ref[SUBLANES:2 * SUBLANES, cols] = prev7
            y = cb_ref[:, cols] + jnp.concatenate(back2, axis=0) * cw_ref[0:1, cols]
            y = y + jnp.concatenate(back1, axis=0) * cw_ref[1:2, cols]
            return y + u * cw_ref[2:3, cols]

        for c in range(D_FF // FFN_COLS):
            a = conv_cols(c * FFN_COLS)
            g = conv_cols(D_FF + c * FFN_COLS)
            gelu = 0.5 * g * (1.0 + jnp.tanh(math.sqrt(2.0 / math.pi) * (g + 0.044715 * (g * g * g))))
            act_ref[rows, c * FFN_COLS:(c + 1) * FFN_COLS] = (gelu * a).astype(BF16)

        out = x + _rms(_dot(act_ref[rows, :], wdown_ref[...]), gpost_ref[...])
        for j in range(n_slabs):
            for g in range(n_groups):
                for v in range(SUBLANES):
                    pr = g * FFN_GROUP + v * SUBLANES
                    dst = pl.ds(g * FFN_GROUP + v, SUBLANES, stride=SUBLANES)
                    ys_ref.at[slab, j][dst, :] = out[pr:pr + SUBLANES, j * LANES:(j + 1) * LANES]
            out_ref[rows, j * LANES:(j + 1) * LANES] = ys_ref[slab, j]


def _ffn(l, x, gpre, wup, cw, cb, wdown, gpost):
    row = lambda i: (i, 0)
    lay = lambda i: (l, 0, 0)
    vec = _resident((None, 1, D_MODEL), lay)
    return pl.pallas_call(
        _ffn_kernel,
        grid=(TOKENS // TM_FFN,),
        in_specs=[
            pl.BlockSpec((TM_FFN, D_MODEL), row),
            vec,
            _resident((None, D_MODEL, 2 * D_FF), lay),
            _resident((None, FFN_CONV, 2 * D_FF), lay),
            _resident((None, 1, 2 * D_FF), lay),
            _resident((None, D_FF, D_MODEL), lay),
            vec,
        ],
        out_specs=pl.BlockSpec((TM_FFN, D_MODEL), row),
        out_shape=jax.ShapeDtypeStruct((TOKENS, D_MODEL), F32),
        scratch_shapes=[
            pltpu.VMEM((2 * SUBLANES, 2 * D_FF), F32),
            pltpu.VMEM((TM_FFN, D_FF), BF16),
            pltpu.VMEM((TM_FFN // SUB_FFN, D_MODEL // LANES, SUB_FFN, LANES), F32),
            pltpu.VMEM((TM_FFN // SUB_FFN, D_MODEL // LANES, SUB_FFN, LANES), F32),
        ],
        compiler_params=_params("arbitrary"),
        name="ffn",
    )(x, gpre, wup, cw, cb, wdown, gpost)


def kernel(x, mem, rel_bias, pre_mix_g, w_in, mconv_w, mconv_b, b_igate, b_fgate, attn_out_g,
           mlstm_out_g, w_out, post_mix_g, pre_mem_g, wq_mem, wk_mem, wv_mem, wo_mem, post_mem_g,
           pre_ffn_g, w_up, fconv_w, fconv_b, w_down, post_ffn_g):
    vec3 = lambda t: t.reshape(DEPTH, 1, -1)

    v0 = 3 * ATTN_WIDTH + 2 * MLSTM_WIDTH
    g0 = v0 + 2 * MLSTM_WIDTH
    w_main = jnp.concatenate([w_in[:, :, :v0], w_in[:, :, v0 + MLSTM_WIDTH:g0]], axis=2).astype(BF16)
    w_vt = jnp.swapaxes(w_in[:, :, v0:v0 + MLSTM_WIDTH], 1, 2).astype(BF16)
    w_gate = jnp.pad(w_in[:, :, g0:], ((0, 0), (0, 0), (0, LANES - GATE_ROWS))).astype(BF16)
    w_gate_t = jnp.pad(jnp.swapaxes(w_in[:, :, g0:], 1, 2), ((0, 0), (0, GATE_ROWS), (0, 0))).astype(BF16)
    gate_bias = jnp.concatenate([b_igate, b_fgate], axis=1)
    gbias = jnp.pad(gate_bias, ((0, 0), (0, LANES - GATE_ROWS)))
    gbias_t = jnp.broadcast_to(gate_bias[:, :, None], (DEPTH, GATE_ROWS, LANES))
    ksc = jnp.concatenate([jnp.ones((1, MLSTM_WIDTH), F32),
                           jnp.full((1, MLSTM_WIDTH), MLSTM_HEAD_DIM ** -0.5, F32)], axis=1)
    w_out_bf = w_out.astype(BF16)
    wq_bf = wq_mem.astype(BF16)
    wo_bf = wo_mem.astype(BF16)
    w_up_bf = w_up.astype(BF16)
    w_down_bf = w_down.astype(BF16)
    bias = _attn_bias_table(rel_bias)

    k_mem, v_mem = _memkv(mem.reshape(BATCH * N_MEM, D_MODEL).astype(BF16), wk_mem, wv_mem)

    xs = x.reshape(TOKENS, D_MODEL)
    for l in range(DEPTH):
        qkv, qk, vt, om, gates, gates_t = _mix_in(l, xs, vec3(pre_mix_g), w_main, w_vt, w_gate,
                                                  w_gate_t, mconv_w, vec3(mconv_b), ksc)
        attn = _dil_attn(qkv, bias)
        hmn = _mlstm(l, qk, vt, om, gates, gates_t, vec3(gbias), gbias_t, vec3(mlstm_out_g))
        xs = _mix_out(l, xs, attn, hmn, vec3(attn_out_g), w_out_bf, vec3(post_mix_g),
                      vec3(pre_mem_g), wq_bf, k_mem, v_mem, wo_bf, vec3(post_mem_g))
        xs = _ffn(l, xs, vec3(pre_ffn_g), w_up_bf, fconv_w, vec3(fconv_b), w_down_bf,
                  vec3(post_ffn_g))
    return xs.reshape(BATCH, SEQ, D_MODEL)
```

```python
import functools
import math

import jax
import jax.numpy as jnp
import numpy as np
from jax import lax
from jax.experimental import pallas as pl
from jax.experimental.pallas import tpu as pltpu

F32 = jnp.float32
BF16 = jnp.bfloat16

D_MODEL = 1024
BATCH = 4
SEQ = 4096
DEPTH = 4
TOKENS = BATCH * SEQ

ATTN_HEAD_DIM = 64
ATTN_WIDTH = 512
N_ATTN_HEADS = 8
DILATIONS = (1, 4, 16)
ATTN_BLK = 128
N_MLSTM_HEADS = 4
MLSTM_WIDTH = 512
MLSTM_HEAD_DIM = 128
MLSTM_CHUNK = 128
MLSTM_CONV = 4
N_BUCKETS = 32
MAX_DISTANCE = 2048
N_MEM = 256
N_XHEADS = 4
XHEAD_DIM = 256
D_FF = 2816
FFN_CONV = 3
EPS = 1e-6

LANES = 128
SUBLANES = 8
NEG = -1e30
VMEM_LIMIT = 56 * 1024 * 1024

TM_IN = 1024
SUB_IN = 256
MIXIN_COLS = 256
TM_OUT = 1024
SUB_OUT = 256
TM_FFN = 512
SUB_FFN = 256
FFN_GROUP = SUBLANES * SUBLANES
FFN_COLS = 256
MAIN_COLS = 3 * ATTN_WIDTH + 2 * MLSTM_WIDTH
GATE_ROWS = 2 * N_MLSTM_HEADS
MLSTM_BATCH = BATCH
MLSTM_OPERAND_ROWS = MLSTM_HEAD_DIM + 16

NT = (((1,), (1,)), ((), ()))
TN = (((0,), (0,)), ((), ()))


def _dot(a, b):
    return jnp.dot(a, b, preferred_element_type=F32)


def _dot_nt(a, b):
    return lax.dot_general(a, b, NT, preferred_element_type=F32)


def _rms(x, g):
    return x * lax.rsqrt(jnp.mean(x * x, axis=-1, keepdims=True) + EPS) * g


def _sigmoid(x):
    return 1.0 / (1.0 + jnp.exp(-x))


def _interleave(gens):
    pending = list(gens)
    active = []
    while pending or active:
        if pending:
            active.append(pending.pop(0))
        for g in list(active):
            try:
                next(g)
            except StopIteration:
                active.remove(g)


def _resident(block_shape, index_map):
    return pl.BlockSpec(block_shape, index_map, pipeline_mode=pl.Buffered(1))


def _params(*semantics):
    return pltpu.CompilerParams(dimension_semantics=semantics, vmem_limit_bytes=VMEM_LIMIT)


def _memkv_kernel(mem_ref, wk_ref, wv_ref, k_ref, v_ref):
    mem = mem_ref[...]
    k_ref[...] = _dot(mem, wk_ref[...].astype(BF16)).astype(BF16)
    v_ref[...] = _dot(mem, wv_ref[...].astype(BF16)).astype(BF16)


def _memkv(mem_bf, wk, wv):
    rows = BATCH * N_MEM
    w_spec = pl.BlockSpec((None, D_MODEL, D_MODEL), lambda l: (l, 0, 0))
    o_spec = pl.BlockSpec((None, rows, D_MODEL), lambda l: (l, 0, 0))
    o_shape = jax.ShapeDtypeStruct((DEPTH, rows, D_MODEL), BF16)
    return pl.pallas_call(
        _memkv_kernel,
        grid=(DEPTH,),
        in_specs=[_resident((rows, D_MODEL), lambda l: (0, 0)), w_spec, w_spec],
        out_specs=[o_spec, o_spec],
        out_shape=[o_shape, o_shape],
        compiler_params=_params("parallel"),
        name="memkv",
    )(mem_bf, wk, wv)


def _mix_in_kernel(x_ref, g_ref, w_ref, wvt_ref, wom_ref, wg_ref, wgt_ref, cw_ref, cb_ref, ksc_ref,
                   qkv_ref, qk_ref, vt_ref, om_ref, gate_ref, gatet_ref, buf_ref):
    @pl.when(pl.program_id(0) % (SEQ // TM_IN) == 0)
    def _():
        buf_ref[0:SUBLANES, :] = jnp.zeros((SUBLANES, 2 * MLSTM_WIDTH), F32)

    a0 = 3 * ATTN_WIDTH
    a1 = a0 + 2 * MLSTM_WIDTH
    cw_cols = MIXIN_COLS
    n_conv = 2 * MLSTM_WIDTH // cw_cols

    def sub_tile(r0):
        rows = slice(r0, r0 + SUB_IN)
        h = _rms(x_ref[rows, :], g_ref[...]).astype(BF16)
        yield

        def plain(out_ref, wsrc_ref, c0):
            out_ref[rows, c0:c0 + cw_cols] = _dot(h, wsrc_ref[:, c0:c0 + cw_cols]).astype(BF16)

        def v_transposed(f0, _):
            vt_ref[f0:f0 + cw_cols, rows] = _dot_nt(wvt_ref[f0:f0 + cw_cols, :], h).astype(BF16)

        plain_jobs = ([(plain, qkv_ref, w_ref, c) for c in range(0, a0, cw_cols)]
                      + [(v_transposed, c, None) for c in range(0, MLSTM_WIDTH, cw_cols)]
                      + [(plain, om_ref, wom_ref, c) for c in range(0, MLSTM_WIDTH, cw_cols)])
        per_conv = -(-len(plain_jobs) // n_conv)
        for cc in range(n_conv):
            cols = slice(cc * cw_cols, (cc + 1) * cw_cols)
            pre = _dot(h, w_ref[:, a0 + cc * cw_cols:a0 + (cc + 1) * cw_cols])
            buf_ref[SUBLANES + r0:SUBLANES + r0 + SUB_IN, cols] = pre
            yield
            for fn, *job in plain_jobs[cc * per_conv:(cc + 1) * per_conv]:
                fn(*job)
            y = cb_ref[:, cols]
            for j in range(MLSTM_CONV - 1):
                lo = SUBLANES + r0 - (MLSTM_CONV - 1) + j
                y = y + buf_ref[lo:lo + SUB_IN, cols] * cw_ref[j:j + 1, cols]
            y = y + pre * cw_ref[MLSTM_CONV - 1:MLSTM_CONV, cols]
            qk_ref[rows, cols] = (y * _sigmoid(y) * ksc_ref[:, cols]).astype(BF16)
            yield
        gate_ref[rows, :] = _dot(h, wg_ref[...])
        gatet_ref[:, rows] = _dot_nt(wgt_ref[...], h)[0:GATE_ROWS, :]

    _interleave([sub_tile(r0) for r0 in range(0, TM_IN, SUB_IN)])
    buf_ref[0:SUBLANES, :] = buf_ref[TM_IN:TM_IN + SUBLANES, :]


def _mix_in(l, x, pre_g, w_main, w_vt, w_om, w_gate, w_gate_t, cw, cb, ksc):
    row = lambda i: (i, 0)
    col = lambda i: (0, i)
    lay = lambda i: (l, 0, 0)
    return pl.pallas_call(
        _mix_in_kernel,
        grid=(TOKENS // TM_IN,),
        in_specs=[
            pl.BlockSpec((TM_IN, D_MODEL), row),
            _resident((None, 1, D_MODEL), lay),
            _resident((None, D_MODEL, MAIN_COLS), lay),
            _resident((None, MLSTM_WIDTH, D_MODEL), lay),
            _resident((None, D_MODEL, MLSTM_WIDTH), lay),
            _resident((None, D_MODEL, LANES), lay),
            _resident((None, 2 * GATE_ROWS, D_MODEL), lay),
            _resident((None, MLSTM_CONV, 2 * MLSTM_WIDTH), lay),
            _resident((None, 1, 2 * MLSTM_WIDTH), lay),
            _resident((1, 2 * MLSTM_WIDTH), lambda i: (0, 0)),
        ],
        out_specs=[
            pl.BlockSpec((TM_IN, 3 * ATTN_WIDTH), row),
            pl.BlockSpec((TM_IN, 2 * MLSTM_WIDTH), row),
            pl.BlockSpec((MLSTM_WIDTH, TM_IN), col),
            pl.BlockSpec((TM_IN, MLSTM_WIDTH), row),
            pl.BlockSpec((TM_IN, LANES), row),
            pl.BlockSpec((GATE_ROWS, TM_IN), col),
        ],
        out_shape=[
            jax.ShapeDtypeStruct((TOKENS, 3 * ATTN_WIDTH), BF16),
            jax.ShapeDtypeStruct((TOKENS, 2 * MLSTM_WIDTH), BF16),
            jax.ShapeDtypeStruct((MLSTM_WIDTH, TOKENS), BF16),
            jax.ShapeDtypeStruct((TOKENS, MLSTM_WIDTH), BF16),
            jax.ShapeDtypeStruct((TOKENS, LANES), F32),
            jax.ShapeDtypeStruct((GATE_ROWS, TOKENS), F32),
        ],
        scratch_shapes=[pltpu.VMEM((TM_IN + SUBLANES, 2 * MLSTM_WIDTH), F32)],
        compiler_params=_params("arbitrary"),
        name="mix_in",
    )(x, pre_g, w_main, w_vt, w_om, w_gate, w_gate_t, cw, cb, ksc)


N_UNITS = SEQ // ATTN_BLK
PIPE_DEPTH = 4
ONES_ROWS = 16


def _div(u, n):
    return lax.shift_right_logical(u, int(math.log2(n)))


def _mod(u, n):
    return lax.bitwise_and(u, n - 1)


def _dil_attn_kernel(q_ref, k_ref, v_ref, bias_ref, out_ref,
                     qf, kf, vf, q4f, k4f, v4f, dq, dk, dvt, o_s0, o_s1, o_s2, l_s0, l_s1, l_s2):
    d1, d4, d16 = DILATIONS
    hd = ATTN_HEAD_DIM
    dvt[LANES:LANES + ONES_ROWS, :] = jnp.ones((ONES_ROWS, SEQ), BF16)

    def widen(r, carry):
        rows = pl.ds(pl.multiple_of(r * ATTN_BLK, ATTN_BLK), ATTN_BLK)
        q = q_ref[rows, :].astype(F32) * (hd ** -0.5)
        k = k_ref[rows, :].astype(F32)
        v = v_ref[rows, :].astype(F32)
        qf[rows, :] = q
        kf[rows, :] = k
        vf[rows, :] = v
        dq[rows, :] = q.astype(BF16)
        dk[rows, :] = k.astype(BF16)
        dvt[0:LANES, rows] = v.T.astype(BF16)
        return carry

    lax.fori_loop(0, N_UNITS, widen, 0, unroll=4)

    lane = lax.broadcasted_iota(jnp.int32, (ATTN_BLK, LANES), 1)
    first_head = lane < hd
    key_in_prev = lax.broadcasted_iota(jnp.int32, (2 * ATTN_BLK, 2 * ATTN_BLK), 0) < ATTN_BLK

    def deinterleave(src_refs, keep_refs, stride, start_of):
        def body(u, carry):
            src = pl.ds(start_of(u), ATTN_BLK, stride=stride)
            dst = pl.ds(pl.multiple_of(u * ATTN_BLK, ATTN_BLK), ATTN_BLK)
            q, k, v = (r[src, :] for r in src_refs)
            if keep_refs is not None:
                for r, val in zip(keep_refs, (q, k, v)):
                    r[dst, :] = val
            dq[dst, :] = q.astype(BF16)
            dk[dst, :] = k.astype(BF16)
            dvt[0:LANES, dst] = v.T.astype(BF16)
            return carry
        lax.fori_loop(0, N_UNITS, body, 0, unroll=4)

    def run_pattern(p, d, o_scr, l_scr):
        nb = N_UNITS // d

        def qk(u):
            cur = pl.ds(u * ATTN_BLK, ATTN_BLK)
            prev = pl.ds(max(u - 1, 0) * ATTN_BLK, ATTN_BLK)
            q = dq[cur, :]
            zero = jnp.zeros_like(q)
            q2 = jnp.concatenate([jnp.where(first_head, q, zero), jnp.where(first_head, zero, q)], axis=0)
            kwin = jnp.concatenate([dk[prev, :], dk[cur, :]], axis=0)
            bias = bias_ref[p]
            if u % nb == 0:
                bias = jnp.where(key_in_prev, NEG, bias)
            return _dot_nt(kwin, q2) + bias

        def rest(u, st):
            cur = pl.ds(u * ATTN_BLK, ATTN_BLK)
            prev = pl.ds(max(u - 1, 0) * ATTN_BLK, ATTN_BLK)
            m = jnp.max(st, axis=0, keepdims=True)
            e = jnp.exp(st - m)
            vtwin = jnp.concatenate([dvt[:, prev], dvt[:, cur]], axis=1)
            ot = _dot(vtwin, e.astype(BF16))
            s = ot[LANES:LANES + 1, :]
            inv = 1.0 / s
            lse = m + jnp.log(s)
            o_t = jnp.concatenate([ot[0:hd, 0:ATTN_BLK] * inv[:, 0:ATTN_BLK],
                                   ot[hd:2 * hd, ATTN_BLK:] * inv[:, ATTN_BLK:]], axis=0)
            l_t = jnp.concatenate([jnp.broadcast_to(lse[:, 0:ATTN_BLK], (hd, ATTN_BLK)),
                                   jnp.broadcast_to(lse[:, ATTN_BLK:], (hd, ATTN_BLK))], axis=0)
            dst = pl.ds(u // nb + (u % nb) * (ATTN_BLK * d), ATTN_BLK, stride=d)
            o_scr[dst, :] = o_t.T
            l_scr[dst, :] = l_t.T

        pending = {}
        for u in range(N_UNITS + PIPE_DEPTH):
            if u < N_UNITS:
                pending[u] = qk(u)
            if u >= PIPE_DEPTH:
                rest(u - PIPE_DEPTH, pending.pop(u - PIPE_DEPTH))

    run_pattern(0, d1, o_s0, l_s0)
    per4 = SEQ // d4
    nb4 = N_UNITS // d4
    deinterleave((qf, kf, vf), (q4f, k4f, v4f), d4,
                 lambda u: _div(u, nb4) + _mod(u, nb4) * (ATTN_BLK * d4))
    run_pattern(1, d4, o_s1, l_s1)
    ratio = d16 // d4
    nb16 = N_UNITS // d16
    deinterleave((q4f, k4f, v4f), None, ratio,
                 lambda u: (_mod(_div(u, nb16), d4) * per4 + _mod(u, nb16) * (ATTN_BLK * ratio)
                            + _div(_div(u, nb16), d4)))
    run_pattern(2, d16, o_s2, l_s2)

    def combine(r, carry):
        rows = pl.ds(pl.multiple_of(r * ATTN_BLK, ATTN_BLK), ATTN_BLK)
        la, lb, lc = l_s0[rows, :], l_s1[rows, :], l_s2[rows, :]
        mx = jnp.maximum(jnp.maximum(la, lb), lc)
        ea, eb, ec = jnp.exp(la - mx), jnp.exp(lb - mx), jnp.exp(lc - mx)
        mixed = (ea * o_s0[rows, :] + eb * o_s1[rows, :] + ec * o_s2[rows, :]) / (ea + eb + ec)
        out_ref[rows, :] = mixed.astype(BF16)
        return carry

    lax.fori_loop(0, N_UNITS, combine, 0, unroll=2)


def _dil_attn(qkv, bias):
    pairs = N_ATTN_HEADS // 2
    blk = (SEQ, LANES)
    f32_seq = pltpu.VMEM((SEQ, LANES), F32)
    bf_seq = pltpu.VMEM((SEQ, LANES), BF16)
    return pl.pallas_call(
        _dil_attn_kernel,
        grid=(BATCH, pairs),
        in_specs=[
            pl.BlockSpec(blk, lambda b, hp: (b, hp)),
            pl.BlockSpec(blk, lambda b, hp: (b, pairs + hp)),
            pl.BlockSpec(blk, lambda b, hp: (b, 2 * pairs + hp)),
            pl.BlockSpec((None, len(DILATIONS), 2 * ATTN_BLK, 2 * ATTN_BLK),
                         lambda b, hp: (hp, 0, 0, 0)),
        ],
        out_specs=pl.BlockSpec(blk, lambda b, hp: (b, hp)),
        out_shape=jax.ShapeDtypeStruct((TOKENS, ATTN_WIDTH), BF16),
        scratch_shapes=([f32_seq] * 6 + [bf_seq] * 2 + [pltpu.VMEM((LANES + ONES_ROWS, SEQ), BF16)]
                        + [f32_seq] * 6),
        compiler_params=_params("parallel", "parallel"),
        name="dil_attn",
    )(qkv, qkv, qkv, bias)


def _t5_causal_bucket(dist):
    max_exact = N_BUCKETS // 2
    d_f = jnp.maximum(dist, 1).astype(F32)
    large = max_exact + (jnp.log(d_f / max_exact) / math.log(MAX_DISTANCE / max_exact)
                         * (N_BUCKETS - max_exact)).astype(jnp.int32)
    large = jnp.minimum(large, N_BUCKETS - 1)
    return jnp.where(dist < max_exact, dist, large)


def _attn_bias_table(rel_bias):
    pairs = N_ATTN_HEADS // 2
    n_pat = len(DILATIONS)
    t = _attn_bias_qk(rel_bias).reshape(n_pat, pairs, 2, ATTN_BLK, 2 * ATTN_BLK)
    return jnp.transpose(t, (1, 0, 4, 2, 3)).reshape(pairs, n_pat, 2 * ATTN_BLK, 2 * ATTN_BLK)


def _attn_bias_qk(rel_bias):
    blk = ATTN_BLK
    period = 3 * blk + 1
    pad_lo = jnp.full((N_ATTN_HEADS, blk - 1), NEG, F32)
    pad_hi = jnp.full((N_ATTN_HEADS, period - 2 * blk), NEG, F32)
    tables = []
    for d in DILATIONS:
        bias_sub = rel_bias[_t5_causal_bucket(jnp.arange(blk + 1) * d)].T.astype(F32)
        r = jnp.concatenate([pad_lo, bias_sub[:, ::-1], pad_hi], axis=1)
        tiled = jnp.broadcast_to(r[:, None, :], (N_ATTN_HEADS, blk, period))
        skew = tiled.reshape(N_ATTN_HEADS, blk * period)[:, :blk * (period - 1)]
        skew = skew.reshape(N_ATTN_HEADS, blk, period - 1)
        tables.append(skew[:, :, blk - 1:3 * blk - 1])
    return jnp.stack(tables)


def _log_sigmoid(x):
    return jnp.minimum(x, 0.0) - jnp.log1p(jnp.exp(-jnp.abs(x)))


def _split2(x):
    hi = x.astype(BF16)
    return hi, (x - hi.astype(F32)).astype(BF16)


def _split3(x):
    hi, mid = _split2(x)
    lo = (x - hi.astype(F32) - mid.astype(F32)).astype(BF16)
    return hi, mid, lo


def _zip_phases(gens):
    pending = list(gens)
    active = []
    while pending or active:
        if pending:
            active.append(pending.pop(0))
        for g in list(active):
            try:
                next(g)
            except StopIteration:
                active.remove(g)
        yield


def _mlstm_kernel(*refs):
    nb = MLSTM_BATCH
    qk_ref, om_ref, gate_ref = refs[0:3]
    vt_refs = refs[3:3 + nb]
    gatet_refs = refs[3 + nb:3 + 2 * nb]
    gbias_ref, gbiast_ref, gout_ref, out_ref, c_scr, n_scr, m_scr = refs[3 + 2 * nb:]

    @pl.when(pl.program_id(1) == 0)
    def _():
        c_scr[...] = jnp.zeros_like(c_scr)
        n_scr[...] = jnp.zeros_like(n_scr)
        m_scr[...] = jnp.zeros_like(m_scr)

    nh = N_MLSTM_HEADS
    ch = MLSTM_CHUNK
    hd = MLSTM_HEAD_DIM
    row = lax.broadcasted_iota(jnp.int32, (ch, LANES), 0)
    lane = lax.broadcasted_iota(jnp.int32, (ch, LANES), 1)
    src_le_tgt = row <= lane
    tri_lower = (row >= lane).astype(BF16)
    tri_upper = src_le_tgt.astype(BF16)
    grow = lax.broadcasted_iota(jnp.int32, (GATE_ROWS, LANES), 0)
    pad_bf = jnp.zeros((MLSTM_OPERAND_ROWS - hd - 2, ch), BF16)

    def batch_row(bi):
        gates = gate_ref[bi] + gbias_ref[...]
        logf = jnp.where((lane >= nh) & (lane < 2 * nh), _log_sigmoid(gates), 0.0)
        hi, mid, lo = _split3(logf)
        bcum = _dot(tri_lower, hi) + _dot(tri_lower, mid) + _dot(tri_lower, lo)
        gates_t = gatet_refs[bi][...] + gbiast_ref[...]
        logf_t = jnp.where(grow >= nh, _log_sigmoid(gates_t), 0.0)
        hi, mid, lo = _split3(logf_t)
        bcum_t = _dot(hi, tri_upper) + _dot(mid, tri_upper) + _dot(lo, tri_upper)
        yield
        outs = [None] * nh

        def head(h):
            si = bi * nh + h
            hs = slice(h * hd, (h + 1) * hd)
            ks = slice(MLSTM_WIDTH + h * hd, MLSTM_WIDTH + (h + 1) * hd)
            q_h = qk_ref[bi, :, hs]
            k_h = qk_ref[bi, :, ks]
            vt_h = vt_refs[bi][hs, :]
            b_row = bcum_t[nh + h:nh + h + 1, :]
            i_row = gates_t[h:h + 1, :]
            w_col = gates[:, h:h + 1] - bcum[:, nh + h:nh + h + 1]
            g = b_row[:, ch - 1:ch]
            m_prev = m_scr[si:si + 1, :]
            s_t = _dot_nt(k_h, q_h)
            c_prev = c_scr[si]
            n_prev = n_scr[si:si + 1, :]
            carried = _dot_nt(jnp.concatenate([c_prev.astype(BF16), *_split2(n_prev), pad_bf], axis=0), q_h)
            yield
            d_t = jnp.where(src_le_tgt, b_row + w_col, NEG)
            a_row = b_row + m_prev
            m_row = jnp.maximum(a_row, jnp.max(d_t, axis=0, keepdims=True))
            p_t = jnp.exp(d_t - m_row) * s_t
            inter = jnp.exp(a_row - m_row)
            yield
            p_sum = jnp.sum(p_t, axis=0, keepdims=True)
            fresh = _dot(vt_h, p_t.astype(BF16))
            w_row = g - b_row + i_row
            m_new = jnp.maximum(g + m_prev, jnp.max(w_row, axis=1, keepdims=True))
            decay = jnp.exp(g + m_prev - m_new)
            wt = jnp.exp(w_row - m_new)
            upd = _dot(jnp.concatenate([(vt_h.astype(F32) * wt).astype(BF16), *_split2(wt), pad_bf], axis=0), k_h)
            c_scr[si] = decay[:, 0:1] * c_prev + upd[0:hd, :]
            n_scr[si:si + 1, :] = decay[:, 0:1] * n_prev + (upd[hd:hd + 1, :] + upd[hd + 1:hd + 2, :])
            m_scr[si:si + 1, :] = m_new
            yield
            num = inter * carried[0:hd, :] + fresh
            den = inter * (carried[hd:hd + 1, :] + carried[hd + 1:hd + 2, :]) + p_sum
            h_t = num * (1.0 / jnp.maximum(jnp.abs(den), jnp.exp(-m_row)))
            outs[h] = _sigmoid(om_ref[bi, :, hs].astype(F32)) * h_t.T

        for _ in _zip_phases([head(h) for h in range(nh)]):
            yield
        hm = jnp.concatenate(outs, axis=1)
        out_ref[bi] = _rms(hm, gout_ref[...]).astype(BF16)

    for _ in _zip_phases([batch_row(bi) for bi in range(nb)]):
        pass


def _mlstm(l, qk, vt, om, gates, gates_t, gbias, gbias_t, gout):
    nc = SEQ // MLSTM_CHUNK
    nb = MLSTM_BATCH
    blk3 = lambda g, c: (g, c, 0)
    lay = lambda g, c: (l, 0, 0)
    col = lambda bi: (lambda g, c: (0, (g * nb + bi) * nc + c))
    three = lambda t: t.reshape(BATCH, SEQ, t.shape[-1])
    n_chains = nb * N_MLSTM_HEADS
    out = pl.pallas_call(
        _mlstm_kernel,
        grid=(BATCH // nb, nc),
        in_specs=[
            pl.BlockSpec((nb, MLSTM_CHUNK, 2 * MLSTM_WIDTH), blk3),
            pl.BlockSpec((nb, MLSTM_CHUNK, MLSTM_WIDTH), blk3),
            pl.BlockSpec((nb, MLSTM_CHUNK, LANES), blk3),
        ] + [pl.BlockSpec((MLSTM_WIDTH, MLSTM_CHUNK), col(bi)) for bi in range(nb)]
          + [pl.BlockSpec((GATE_ROWS, MLSTM_CHUNK), col(bi)) for bi in range(nb)] + [
            _resident((None, 1, LANES), lay),
            _resident((None, GATE_ROWS, LANES), lay),
            _resident((None, 1, MLSTM_WIDTH), lay),
        ],
        out_specs=pl.BlockSpec((nb, MLSTM_CHUNK, MLSTM_WIDTH), blk3),
        out_shape=jax.ShapeDtypeStruct((BATCH, SEQ, MLSTM_WIDTH), BF16),
        scratch_shapes=[
            pltpu.VMEM((n_chains, MLSTM_HEAD_DIM, MLSTM_HEAD_DIM), F32),
            pltpu.VMEM((n_chains, MLSTM_HEAD_DIM), F32),
            pltpu.VMEM((n_chains, LANES), F32),
        ],
        compiler_params=_params("arbitrary", "arbitrary"),
        name="mlstm",
    )(three(qk), three(om), three(gates), *([vt] * nb), *([gates_t] * nb), gbias, gbias_t, gout)
    return out.reshape(TOKENS, MLSTM_WIDTH)


def _mix_out_kernel(x_ref, attn_ref, hmn_ref, gattn_ref, wout_ref, gpost_ref,
                    gpre_ref, wq_ref, k_ref, v_ref, wo_ref, gpostm_ref, out_ref):
    def sub_tile(r0):
        rows = slice(r0, r0 + SUB_OUT)
        an = _rms(attn_ref[rows, :].astype(F32), gattn_ref[...]).astype(BF16)
        yield
        h = _dot(an, wout_ref[0:ATTN_WIDTH, :]) + _dot(hmn_ref[rows, :], wout_ref[ATTN_WIDTH:, :])
        yield
        x1 = x_ref[rows, :] + _rms(h, gpost_ref[...])
        hq = _rms(x1, gpre_ref[...]).astype(BF16)
        yield
        q = (_dot(hq, wq_ref[...]) * (XHEAD_DIM ** -0.5)).astype(BF16)
        yield
        heads = []
        for hd in range(N_XHEADS):
            sl = slice(hd * XHEAD_DIM, (hd + 1) * XHEAD_DIM)
            logits = _dot_nt(q[:, sl], k_ref[:, sl])
            yield
            m = jnp.max(logits, axis=1, keepdims=True)
            e = jnp.exp(logits - m)
            s = jnp.sum(e, axis=1, keepdims=True)
            yield
            heads.append((_dot(e.astype(BF16), v_ref[:, sl]) / s).astype(BF16))
        o = jnp.concatenate(heads, axis=1)
        yield
        h2 = _dot(o, wo_ref[...])
        yield
        out_ref[rows, :] = x1 + _rms(h2, gpostm_ref[...])

    _interleave([sub_tile(r0) for r0 in range(0, TM_OUT, SUB_OUT)])


def _mix_out(l, x, attn, hmn, gattn, wout, gpost, gpre, wq, k_mem, v_mem, wo, gpostm):
    row = lambda i: (i, 0)
    lay = lambda i: (l, 0, 0)
    tiles_per_seq = SEQ // TM_OUT
    vec = _resident((None, 1, D_MODEL), lay)
    mat = _resident((None, D_MODEL, D_MODEL), lay)
    return pl.pallas_call(
        _mix_out_kernel,
        grid=(TOKENS // TM_OUT,),
        in_specs=[
            pl.BlockSpec((TM_OUT, D_MODEL), row),
            pl.BlockSpec((TM_OUT, ATTN_WIDTH), row),
            pl.BlockSpec((TM_OUT, MLSTM_WIDTH), row),
            _resident((None, 1, ATTN_WIDTH), lay),
            mat, vec, vec, mat,
            pl.BlockSpec((None, N_MEM, D_MODEL), lambda i: (l, i // tiles_per_seq, 0)),
            pl.BlockSpec((None, N_MEM, D_MODEL), lambda i: (l, i // tiles_per_seq, 0)),
            mat, vec,
        ],
        out_specs=pl.BlockSpec((TM_OUT, D_MODEL), row),
        out_shape=jax.ShapeDtypeStruct((TOKENS, D_MODEL), F32),
        compiler_params=_params("parallel"),
        name="mix_out",
    )(x, attn, hmn, gattn, wout, gpost, gpre, wq, k_mem, v_mem, wo, gpostm)


def _ffn_kernel(x_ref, gpre_ref, wup_ref, cw_ref, cb_ref, wdown_ref, gpost_ref, out_ref,
                carry_ref, act_ref, xs_ref, ys_ref):
    assert FFN_CONV == 3

    @pl.when(pl.program_id(0) % (SEQ // TM_FFN) == 0)
    def _():
        carry_ref[...] = jnp.zeros_like(carry_ref)

    n_groups = SUB_FFN // FFN_GROUP
    n_slabs = D_MODEL // LANES
    last_sub = lax.broadcasted_iota(jnp.int32, (SUBLANES, FFN_COLS), 0) == SUBLANES - 1

    for slab, r0 in enumerate(range(0, TM_FFN, SUB_FFN)):
        rows = slice(r0, r0 + SUB_FFN)
        for j in range(n_slabs):
            xs_ref[slab, j] = x_ref[rows, j * LANES:(j + 1) * LANES]
        x = jnp.concatenate([
            jnp.concatenate([xs_ref.at[slab, j][pl.ds(g * FFN_GROUP + v, SUBLANES, stride=SUBLANES), :]
                             for g in range(n_groups) for v in range(SUBLANES)], axis=0)
            for j in range(n_slabs)], axis=1)
        h = _rms(x, gpre_ref[...]).astype(BF16)

        def conv_cols(c0):
            cols = slice(c0, c0 + FFN_COLS)
            u = _dot(h, wup_ref[:, cols])
            vregs = [[u[g * FFN_GROUP + v * SUBLANES:g * FFN_GROUP + (v + 1) * SUBLANES, :]
                      for v in range(SUBLANES)] for g in range(n_groups)]
            prev6 = carry_ref[0:SUBLANES, cols]
            prev7 = carry_ref[SUBLANES:2 * SUBLANES, cols]
            back1, back2 = [], []
            for g in range(n_groups):
                a6 = pltpu.roll(jnp.where(last_sub, prev6, vregs[g][6]), 1, axis=0)
                a7 = pltpu.roll(jnp.where(last_sub, prev7, vregs[g][7]), 1, axis=0)
                back1 += [a7] + vregs[g][0:7]
                back2 += [a6, a7] + vregs[g][0:6]
                prev6, prev7 = vregs[g][6], vregs[g][7]
            carry_ref[0:SUBLANES, cols] = prev6
            carry_ref[SUBLANES:2 * SUBLANES, cols] = prev7
            y = cb_ref[:, cols] + jnp.concatenate(back2, axis=0) * cw_ref[0:1, cols]
            y = y + jnp.concatenate(back1, axis=0) * cw_ref[1:2, cols]
            return y + u * cw_ref[2:3, cols]

        for c in range(D_FF // FFN_COLS):
            a = conv_cols(c * FFN_COLS)
            g = conv_cols(D_FF + c * FFN_COLS)
            gelu = 0.5 * g * (1.0 + jnp.tanh(math.sqrt(2.0 / math.pi) * (g + 0.044715 * (g * g * g))))
            act_ref[rows, c * FFN_COLS:(c + 1) * FFN_COLS] = (gelu * a).astype(BF16)

        out = x + _rms(_dot(act_ref[rows, :], wdown_ref[...]), gpost_ref[...])
        for j in range(n_slabs):
            for g in range(n_groups):
                for v in range(SUBLANES):
                    pr = g * FFN_GROUP + v * SUBLANES
                    dst = pl.ds(g * FFN_GROUP + v, SUBLANES, stride=SUBLANES)
                    ys_ref.at[slab, j][dst, :] = out[pr:pr + SUBLANES, j * LANES:(j + 1) * LANES]
            out_ref[rows, j * LANES:(j + 1) * LANES] = ys_ref[slab, j]


def _ffn(l, x, gpre, wup, cw, cb, wdown, gpost):
    row = lambda i: (i, 0)
    lay = lambda i: (l, 0, 0)
    vec = _resident((None, 1, D_MODEL), lay)
    return pl.pallas_call(
        _ffn_kernel,
        grid=(TOKENS // TM_FFN,),
        in_specs=[
            pl.BlockSpec((TM_FFN, D_MODEL), row),
            vec,
            _resident((None, D_MODEL, 2 * D_FF), lay),
            _resident((None, FFN_CONV, 2 * D_FF), lay),
            _resident((None, 1, 2 * D_FF), lay),
            _resident((None, D_FF, D_MODEL), lay),
            vec,
        ],
        out_specs=pl.BlockSpec((TM_FFN, D_MODEL), row),
        out_shape=jax.ShapeDtypeStruct((TOKENS, D_MODEL), F32),
        scratch_shapes=[
            pltpu.VMEM((2 * SUBLANES, 2 * D_FF), F32),
            pltpu.VMEM((TM_FFN, D_FF), BF16),
            pltpu.VMEM((TM_FFN // SUB_FFN, D_MODEL // LANES, SUB_FFN, LANES), F32),
            pltpu.VMEM((TM_FFN // SUB_FFN, D_MODEL // LANES, SUB_FFN, LANES), F32),
        ],
        compiler_params=_params("arbitrary"),
        name="ffn",
    )(x, gpre, wup, cw, cb, wdown, gpost)


def kernel(x, mem, rel_bias, pre_mix_g, w_in, mconv_w, mconv_b, b_igate, b_fgate, attn_out_g,
           mlstm_out_g, w_out, post_mix_g, pre_mem_g, wq_mem, wk_mem, wv_mem, wo_mem, post_mem_g,
           pre_ffn_g, w_up, fconv_w, fconv_b, w_down, post_ffn_g):
    vec3 = lambda t: t.reshape(DEPTH, 1, -1)

    v0 = 3 * ATTN_WIDTH + 2 * MLSTM_WIDTH
    g0 = v0 + 2 * MLSTM_WIDTH
    w_main = w_in[:, :, :v0].astype(BF16)
    w_om = w_in[:, :, v0 + MLSTM_WIDTH:g0].astype(BF16)
    w_vt = jnp.swapaxes(w_in[:, :, v0:v0 + MLSTM_WIDTH], 1, 2).astype(BF16)
    w_gate = jnp.pad(w_in[:, :, g0:], ((0, 0), (0, 0), (0, LANES - GATE_ROWS))).astype(BF16)
    w_gate_t = jnp.pad(jnp.swapaxes(w_in[:, :, g0:], 1, 2), ((0, 0), (0, GATE_ROWS), (0, 0))).astype(BF16)
    gate_bias = jnp.concatenate([b_igate, b_fgate], axis=1)
    gbias = jnp.pad(gate_bias, ((0, 0), (0, LANES - GATE_ROWS)))
    gbias_t = jnp.broadcast_to(gate_bias[:, :, None], (DEPTH, GATE_ROWS, LANES))
    ksc = jnp.concatenate([jnp.ones((1, MLSTM_WIDTH), F32),
                           jnp.full((1, MLSTM_WIDTH), MLSTM_HEAD_DIM ** -0.5, F32)], axis=1)
    w_out_bf = w_out.astype(BF16)
    wq_bf = wq_mem.astype(BF16)
    wo_bf = wo_mem.astype(BF16)
    w_up_bf = w_up.astype(BF16)
    w_down_bf = w_down.astype(BF16)
    bias = _attn_bias_table(rel_bias)

    k_mem, v_mem = _memkv(mem.reshape(BATCH * N_MEM, D_MODEL).astype(BF16), wk_mem, wv_mem)

    xs = x.reshape(TOKENS, D_MODEL)
    for l in range(DEPTH):
        qkv, qk, vt, om, gates, gates_t = _mix_in(l, xs, vec3(pre_mix_g), w_main, w_vt, w_om, w_gate,
                                                  w_gate_t, mconv_w, vec3(mconv_b), ksc)
        attn = _dil_attn(qkv, bias)
        hmn = _mlstm(l, qk, vt, om, gates, gates_t, vec3(gbias), gbias_t, vec3(mlstm_out_g))
        xs = _mix_out(l, xs, attn, hmn, vec3(attn_out_g), w_out_bf, vec3(post_mix_g),
                      vec3(pre_mem_g), wq_bf, k_mem, v_mem, wo_bf, vec3(post_mem_g))
        xs = _ffn(l, xs, vec3(pre_ffn_g), w_up_bf, fconv_w, vec3(fconv_b), w_down_bf,
                  vec3(post_ffn_g))
    return xs.reshape(BATCH, SEQ, D_MODEL)
```

```python
import functools
import math

import jax
import jax.numpy as jnp
import numpy as np
from jax import lax
from jax.experimental import pallas as pl
from jax.experimental.pallas import tpu as pltpu

F32 = jnp.float32
BF16 = jnp.bfloat16

D_MODEL = 1024
BATCH = 4
SEQ = 4096
DEPTH = 4
TOKENS = BATCH * SEQ

ATTN_HEAD_DIM = 64
ATTN_WIDTH = 512
N_ATTN_HEADS = 8
DILATIONS = (1, 4, 16)
ATTN_BLK = 128
N_MLSTM_HEADS = 4
MLSTM_WIDTH = 512
MLSTM_HEAD_DIM = 128
MLSTM_CHUNK = 128
MLSTM_CONV = 4
N_BUCKETS = 32
MAX_DISTANCE = 2048
N_MEM = 256
N_XHEADS = 4
XHEAD_DIM = 256
D_FF = 2816
FFN_CONV = 3
EPS = 1e-6

LANES = 128
SUBLANES = 8
NEG = -math.inf
VMEM_LIMIT = 56 * 1024 * 1024

TM_IN = 1024
SUB_IN = 256
MIXIN_COLS = 256
TM_OUT = 1024
SUB_OUT = 256
TM_FFN = 512
SUB_FFN = 256
FFN_GROUP = SUBLANES * SUBLANES
FFN_COLS = 256
MAIN_COLS = 3 * ATTN_WIDTH + 2 * MLSTM_WIDTH
GATE_ROWS = 2 * N_MLSTM_HEADS
MLSTM_BATCH = BATCH
MLSTM_OPERAND_ROWS = MLSTM_HEAD_DIM + 16

NT = (((1,), (1,)), ((), ()))


def _dot(a, b):
    return jnp.dot(a, b, preferred_element_type=F32)


def _dot_nt(a, b):
    return lax.dot_general(a, b, NT, preferred_element_type=F32)


def _rms(x, g):
    return x * lax.rsqrt(jnp.mean(x * x, axis=-1, keepdims=True) + EPS) * g


def _sigmoid(x):
    return 1.0 / (1.0 + jnp.exp(-x))


def _interleave(gens):
    pending = list(gens)
    active = []
    while pending or active:
        if pending:
            active.append(pending.pop(0))
        for g in list(active):
            try:
                next(g)
            except StopIteration:
                active.remove(g)


def _resident(block_shape, index_map):
    return pl.BlockSpec(block_shape, index_map, pipeline_mode=pl.Buffered(1))


def _params(*semantics):
    return pltpu.CompilerParams(dimension_semantics=semantics, vmem_limit_bytes=VMEM_LIMIT)


def _memkv_kernel(mem_ref, wk_ref, wv_ref, k_ref, v_ref):
    mem = mem_ref[...]
    k_ref[...] = _dot(mem, wk_ref[...].astype(BF16)).astype(BF16)
    v_ref[...] = _dot(mem, wv_ref[...].astype(BF16)).astype(BF16)


def _memkv(mem_bf, wk, wv):
    rows = BATCH * N_MEM
    w_spec = pl.BlockSpec((None, D_MODEL, D_MODEL), lambda l: (l, 0, 0))
    o_spec = pl.BlockSpec((None, rows, D_MODEL), lambda l: (l, 0, 0))
    o_shape = jax.ShapeDtypeStruct((DEPTH, rows, D_MODEL), BF16)
    return pl.pallas_call(
        _memkv_kernel,
        grid=(DEPTH,),
        in_specs=[_resident((rows, D_MODEL), lambda l: (0, 0)), w_spec, w_spec],
        out_specs=[o_spec, o_spec],
        out_shape=[o_shape, o_shape],
        compiler_params=_params("parallel"),
        name="memkv",
    )(mem_bf, wk, wv)


def _mix_in_kernel(x_ref, g_ref, w_ref, wvt_ref, wom_ref, wg_ref, wgt_ref, cw_ref, cb_ref, ksc_ref,
                   qkv_ref, qkwide_ref, qk_ref, vt_ref, om_ref, gate_ref, gatet_ref, buf_ref):
    @pl.when(pl.program_id(0) % (SEQ // TM_IN) == 0)
    def _():
        buf_ref[0:SUBLANES, :] = jnp.zeros((SUBLANES, 2 * MLSTM_WIDTH), F32)

    a0 = 3 * ATTN_WIDTH
    cw_cols = MIXIN_COLS
    n_conv = 2 * MLSTM_WIDTH // cw_cols

    def sub_tile(r0):
        rows = slice(r0, r0 + SUB_IN)
        h = _rms(x_ref[rows, :], g_ref[...]).astype(BF16)
        yield

        def plain(out_ref, wsrc_ref, c0):
            res = _dot(h, wsrc_ref[:, c0:c0 + cw_cols]).astype(BF16)
            out_ref[rows, c0:c0 + cw_cols] = res
            if out_ref is qkv_ref and c0 < 2 * ATTN_WIDTH:
                qkwide_ref[rows, c0:c0 + cw_cols] = res.astype(F32)

        def v_transposed(f0, _):
            vt_ref[f0:f0 + cw_cols, rows] = _dot_nt(wvt_ref[f0:f0 + cw_cols, :], h).astype(BF16)

        plain_jobs = ([(plain, qkv_ref, w_ref, c) for c in range(0, a0, cw_cols)]
                      + [(v_transposed, c, None) for c in range(0, MLSTM_WIDTH, cw_cols)]
                      + [(plain, om_ref, wom_ref, c) for c in range(0, MLSTM_WIDTH, cw_cols)])
        per_conv = -(-len(plain_jobs) // n_conv)
        for cc in range(n_conv):
            cols = slice(cc * cw_cols, (cc + 1) * cw_cols)
            pre = _dot(h, w_ref[:, a0 + cc * cw_cols:a0 + (cc + 1) * cw_cols])
            buf_ref[SUBLANES + r0:SUBLANES + r0 + SUB_IN, cols] = pre
            yield
            for fn, *job in plain_jobs[cc * per_conv:(cc + 1) * per_conv]:
                fn(*job)
            y = cb_ref[:, cols]
            for j in range(MLSTM_CONV - 1):
                lo = SUBLANES + r0 - (MLSTM_CONV - 1) + j
                y = y + buf_ref[lo:lo + SUB_IN, cols] * cw_ref[j:j + 1, cols]
            y = y + pre * cw_ref[MLSTM_CONV - 1:MLSTM_CONV, cols]
            qk_ref[rows, cols] = (y * _sigmoid(y) * ksc_ref[:, cols]).astype(BF16)
            yield
        gate_ref[rows, :] = _dot(h, wg_ref[...])
        gatet_ref[:, rows] = _dot_nt(wgt_ref[...], h)[0:GATE_ROWS, :]

    _interleave([sub_tile(r0) for r0 in range(0, TM_IN, SUB_IN)])
    buf_ref[0:SUBLANES, :] = buf_ref[TM_IN:TM_IN + SUBLANES, :]


def _mix_in(l, x, pre_g, w_main, w_vt, w_om, w_gate, w_gate_t, cw, cb, ksc):
    row = lambda i: (i, 0)
    col = lambda i: (0, i)
    lay = lambda i: (l, 0, 0)
    return pl.pallas_call(
        _mix_in_kernel,
        grid=(TOKENS // TM_IN,),
        in_specs=[
            pl.BlockSpec((TM_IN, D_MODEL), row),
            _resident((None, 1, D_MODEL), lay),
            _resident((None, D_MODEL, MAIN_COLS), lay),
            _resident((None, MLSTM_WIDTH, D_MODEL), lay),
            _resident((None, D_MODEL, MLSTM_WIDTH), lay),
            _resident((None, D_MODEL, LANES), lay),
            _resident((None, 2 * GATE_ROWS, D_MODEL), lay),
            _resident((None, MLSTM_CONV, 2 * MLSTM_WIDTH), lay),
            _resident((None, 1, 2 * MLSTM_WIDTH), lay),
            _resident((1, 2 * MLSTM_WIDTH), lambda i: (0, 0)),
        ],
        out_specs=[
            pl.BlockSpec((TM_IN, 3 * ATTN_WIDTH), row),
            pl.BlockSpec((TM_IN, 2 * ATTN_WIDTH), row),
            pl.BlockSpec((TM_IN, 2 * MLSTM_WIDTH), row),
            pl.BlockSpec((MLSTM_WIDTH, TM_IN), col),
            pl.BlockSpec((TM_IN, MLSTM_WIDTH), row),
            pl.BlockSpec((TM_IN, LANES), row),
            pl.BlockSpec((GATE_ROWS, TM_IN), col),
        ],
        out_shape=[
            jax.ShapeDtypeStruct((TOKENS, 3 * ATTN_WIDTH), BF16),
            jax.ShapeDtypeStruct((TOKENS, 2 * ATTN_WIDTH), F32),
            jax.ShapeDtypeStruct((TOKENS, 2 * MLSTM_WIDTH), BF16),
            jax.ShapeDtypeStruct((MLSTM_WIDTH, TOKENS), BF16),
            jax.ShapeDtypeStruct((TOKENS, MLSTM_WIDTH), BF16),
            jax.ShapeDtypeStruct((TOKENS, LANES), F32),
            jax.ShapeDtypeStruct((GATE_ROWS, TOKENS), F32),
        ],
        scratch_shapes=[pltpu.VMEM((TM_IN + SUBLANES, 2 * MLSTM_WIDTH), F32)],
        compiler_params=_params("arbitrary"),
        name="mix_in",
    )(x, pre_g, w_main, w_vt, w_om, w_gate, w_gate_t, cw, cb, ksc)


N_UNITS = SEQ // ATTN_BLK
PIPE_DEPTH = 4
ONES_ROWS = 16


def _div(u, n):
    return lax.shift_right_logical(u, int(math.log2(n)))


def _mod(u, n):
    return lax.bitwise_and(u, n - 1)


def _dil_attn_kernel(q_ref, k_ref, v_ref, qf, kf, bias_ref, out_ref,
                     vf, q4f, k4f, v4f, dq, dk, dvt, o_s0, o_s1, o_s2, l_s0, l_s1, l_s2):
    d1, d4, d16 = DILATIONS
    hd = ATTN_HEAD_DIM
    dvt[LANES:LANES + ONES_ROWS, :] = jnp.ones((ONES_ROWS, SEQ), BF16)

    def widen(r, carry):
        rows = pl.ds(pl.multiple_of(r * ATTN_BLK, ATTN_BLK), ATTN_BLK)
        v = v_ref[rows, :].astype(F32)
        vf[rows, :] = v
        dvt[0:LANES, rows] = v.T.astype(BF16)
        return carry

    lax.fori_loop(0, N_UNITS, widen, 0, unroll=8)

    lane = lax.broadcasted_iota(jnp.int32, (ATTN_BLK, LANES), 1)
    first_head = lane < hd
    key_in_prev = lax.broadcasted_iota(jnp.int32, (2 * ATTN_BLK, 2 * ATTN_BLK), 0) < ATTN_BLK

    def deinterleave(src_refs, keep_refs, stride, start_of):
        def body(u, carry):
            src = pl.ds(start_of(u), ATTN_BLK, stride=stride)
            dst = pl.ds(pl.multiple_of(u * ATTN_BLK, ATTN_BLK), ATTN_BLK)
            q, k, v = (r[src, :] for r in src_refs)
            if keep_refs is not None:
                for r, val in zip(keep_refs, (q, k, v)):
                    r[dst, :] = val
            dq[dst, :] = q.astype(BF16)
            dk[dst, :] = k.astype(BF16)
            dvt[0:LANES, dst] = v.T.astype(BF16)
            return carry
        lax.fori_loop(0, N_UNITS, body, 0, unroll=8)

    def run_pattern(p, d, q_src, k_src, o_scr, l_scr):
        nb = N_UNITS // d

        def qk(u):
            cur = pl.ds(u * ATTN_BLK, ATTN_BLK)
            prev = pl.ds(max(u - 1, 0) * ATTN_BLK, ATTN_BLK)
            q = q_src[cur, :]
            zero = jnp.zeros_like(q)
            q2 = jnp.concatenate([jnp.where(first_head, q, zero), jnp.where(first_head, zero, q)], axis=0)
            kwin = jnp.concatenate([k_src[prev, :], k_src[cur, :]], axis=0)
            bias = bias_ref[p]
            if u % nb == 0:
                bias = jnp.where(key_in_prev, NEG, bias)
            return _dot_nt(kwin, q2) + bias

        def rest(u, st):
            cur = pl.ds(u * ATTN_BLK, ATTN_BLK)
            prev = pl.ds(max(u - 1, 0) * ATTN_BLK, ATTN_BLK)
            m = jnp.max(st, axis=0, keepdims=True)
            e = jnp.exp(st - m)
            vtwin = jnp.concatenate([dvt[:, prev], dvt[:, cur]], axis=1)
            ot = _dot(vtwin, e.astype(BF16))
            s = ot[LANES:LANES + 1, :]
            inv = 1.0 / s
            lse = m + jnp.log(s)
            o_t = jnp.concatenate([ot[0:hd, 0:ATTN_BLK] * inv[:, 0:ATTN_BLK],
                                   ot[hd:2 * hd, ATTN_BLK:] * inv[:, ATTN_BLK:]], axis=0)
            l_t = jnp.concatenate([jnp.broadcast_to(lse[:, 0:ATTN_BLK], (hd, ATTN_BLK)),
                                   jnp.broadcast_to(lse[:, ATTN_BLK:], (hd, ATTN_BLK))], axis=0)
            dst = pl.ds(u // nb + (u % nb) * (ATTN_BLK * d), ATTN_BLK, stride=d)
            o_scr[dst, :] = o_t.T
            l_scr[dst, :] = l_t.T

        pending = {}
        for u in range(N_UNITS + PIPE_DEPTH):
            if u < N_UNITS:
                pending[u] = qk(u)
            if u >= PIPE_DEPTH:
                rest(u - PIPE_DEPTH, pending.pop(u - PIPE_DEPTH))

    run_pattern(0, d1, q_ref, k_ref, o_s0, l_s0)
    per4 = SEQ // d4
    nb4 = N_UNITS // d4
    deinterleave((qf, kf, vf), (q4f, k4f, v4f), d4,
                 lambda u: _div(u, nb4) + _mod(u, nb4) * (ATTN_BLK * d4))
    run_pattern(1, d4, dq, dk, o_s1, l_s1)
    ratio = d16 // d4
    nb16 = N_UNITS // d16
    deinterleave((q4f, k4f, v4f), None, ratio,
                 lambda u: (_mod(_div(u, nb16), d4) * per4 + _mod(u, nb16) * (ATTN_BLK * ratio)
                            + _div(_div(u, nb16), d4)))
    run_pattern(2, d16, dq, dk, o_s2, l_s2)

    def combine(r, carry):
        rows = pl.ds(pl.multiple_of(r * ATTN_BLK, ATTN_BLK), ATTN_BLK)
        la, lb, lc = l_s0[rows, :], l_s1[rows, :], l_s2[rows, :]
        mx = jnp.maximum(jnp.maximum(la, lb), lc)
        ea, eb, ec = jnp.exp(la - mx), jnp.exp(lb - mx), jnp.exp(lc - mx)
        mixed = (ea * o_s0[rows, :] + eb * o_s1[rows, :] + ec * o_s2[rows, :]) / (ea + eb + ec)
        out_ref[rows, :] = mixed.astype(BF16)
        return carry

    lax.fori_loop(0, N_UNITS, combine, 0, unroll=2)


def _dil_attn(qkv, qk_wide, bias):
    pairs = N_ATTN_HEADS // 2
    blk = (SEQ, LANES)
    f32_seq = pltpu.VMEM((SEQ, LANES), F32)
    bf_seq = pltpu.VMEM((SEQ, LANES), BF16)
    return pl.pallas_call(
        _dil_attn_kernel,
        grid=(BATCH, pairs),
        in_specs=[
            pl.BlockSpec(blk, lambda b, hp: (b, hp)),
            pl.BlockSpec(blk, lambda b, hp: (b, pairs + hp)),
            pl.BlockSpec(blk, lambda b, hp: (b, 2 * pairs + hp)),
            pl.BlockSpec(blk, lambda b, hp: (b, hp)),
            pl.BlockSpec(blk, lambda b, hp: (b, pairs + hp)),
            pl.BlockSpec((None, len(DILATIONS), 2 * ATTN_BLK, 2 * ATTN_BLK),
                         lambda b, hp: (hp, 0, 0, 0)),
        ],
        out_specs=pl.BlockSpec(blk, lambda b, hp: (b, hp)),
        out_shape=jax.ShapeDtypeStruct((TOKENS, ATTN_WIDTH), BF16),
        scratch_shapes=([f32_seq] * 4 + [bf_seq] * 2 + [pltpu.VMEM((LANES + ONES_ROWS, SEQ), BF16)]
                        + [f32_seq] * 6),
        compiler_params=_params("parallel", "parallel"),
        name="dil_attn",
    )(qkv, qkv, qkv, qk_wide, qk_wide, bias)


def _t5_causal_bucket(dist):
    max_exact = N_BUCKETS // 2
    d_f = jnp.maximum(dist, 1).astype(F32)
    large = max_exact + (jnp.log(d_f / max_exact) / math.log(MAX_DISTANCE / max_exact)
                         * (N_BUCKETS - max_exact)).astype(jnp.int32)
    large = jnp.minimum(large, N_BUCKETS - 1)
    return jnp.where(dist < max_exact, dist, large)


def _attn_bias_table(rel_bias):
    pairs = N_ATTN_HEADS // 2
    n_pat = len(DILATIONS)
    t = _attn_bias_qk(rel_bias).reshape(n_pat, pairs, 2, ATTN_BLK, 2 * ATTN_BLK)
    return jnp.transpose(t, (1, 0, 4, 2, 3)).reshape(pairs, n_pat, 2 * ATTN_BLK, 2 * ATTN_BLK)


def _attn_bias_qk(rel_bias):
    blk = ATTN_BLK
    period = 3 * blk + 1
    pad_lo = jnp.full((N_ATTN_HEADS, blk - 1), NEG, F32)
    pad_hi = jnp.full((N_ATTN_HEADS, period - 2 * blk), NEG, F32)
    tables = []
    for d in DILATIONS:
        bias_sub = rel_bias[_t5_causal_bucket(jnp.arange(blk + 1) * d)].T.astype(F32)
        r = jnp.concatenate([pad_lo, bias_sub[:, ::-1], pad_hi], axis=1)
        tiled = jnp.broadcast_to(r[:, None, :], (N_ATTN_HEADS, blk, period))
        skew = tiled.reshape(N_ATTN_HEADS, blk * period)[:, :blk * (period - 1)]
        skew = skew.reshape(N_ATTN_HEADS, blk, period - 1)
        tables.append(skew[:, :, blk - 1:3 * blk - 1])
    return jnp.stack(tables)


def _log_sigmoid(x):
    return jnp.minimum(x, 0.0) - jnp.log1p(jnp.exp(-jnp.abs(x)))


def _split2(x):
    hi = x.astype(BF16)
    return hi, (x - hi.astype(F32)).astype(BF16)


def _split3(x):
    hi, mid = _split2(x)
    lo = (x - hi.astype(F32) - mid.astype(F32)).astype(BF16)
    return hi, mid, lo


def _zip_phases(gens):
    pending = list(gens)
    active = []
    while pending or active:
        if pending:
            active.append(pending.pop(0))
        for g in list(active):
            try:
                next(g)
            except StopIteration:
                active.remove(g)
        yield


def _mlstm_kernel(*refs):
    nb = MLSTM_BATCH
    qk_ref, om_ref, gate_ref = refs[0:3]
    vt_refs = refs[3:3 + nb]
    gatet_refs = refs[3 + nb:3 + 2 * nb]
    gbias_ref, gbiast_ref, gout_ref, out_ref, c_scr, n_scr, m_scr = refs[3 + 2 * nb:]

    @pl.when(pl.program_id(1) == 0)
    def _():
        c_scr[...] = jnp.zeros_like(c_scr)
        n_scr[...] = jnp.zeros_like(n_scr)
        m_scr[...] = jnp.zeros_like(m_scr)

    nh = N_MLSTM_HEADS
    ch = MLSTM_CHUNK
    hd = MLSTM_HEAD_DIM
    row = lax.broadcasted_iota(jnp.int32, (ch, LANES), 0)
    lane = lax.broadcasted_iota(jnp.int32, (ch, LANES), 1)
    src_le_tgt = row <= lane
    tri_lower = (row >= lane).astype(BF16)
    tri_upper = src_le_tgt.astype(BF16)
    grow = lax.broadcasted_iota(jnp.int32, (GATE_ROWS, LANES), 0)
    pad_bf = jnp.zeros((MLSTM_OPERAND_ROWS - hd - 2, ch), BF16)

    def batch_row(bi):
        gates = gate_ref[bi] + gbias_ref[...]
        logf = jnp.where((lane >= nh) & (lane < 2 * nh), _log_sigmoid(gates), 0.0)
        hi, mid, lo = _split3(logf)
        bcum = _dot(tri_lower, hi) + _dot(tri_lower, mid) + _dot(tri_lower, lo)
        gates_t = gatet_refs[bi][...] + gbiast_ref[...]
        logf_t = jnp.where(grow >= nh, _log_sigmoid(gates_t), 0.0)
        hi, mid, lo = _split3(logf_t)
        bcum_t = _dot(hi, tri_upper) + _dot(mid, tri_upper) + _dot(lo, tri_upper)
        yield
        outs = [None] * nh

        def head(h):
            si = bi * nh + h
            hs = slice(h * hd, (h + 1) * hd)
            ks = slice(MLSTM_WIDTH + h * hd, MLSTM_WIDTH + (h + 1) * hd)
            q_h = qk_ref[bi, :, hs]
            k_h = qk_ref[bi, :, ks]
            vt_h = vt_refs[bi][hs, :]
            b_row = bcum_t[nh + h:nh + h + 1, :]
            i_row = gates_t[h:h + 1, :]
            w_col = gates[:, h:h + 1] - bcum[:, nh + h:nh + h + 1]
            g = b_row[:, ch - 1:ch]
            m_prev = m_scr[si:si + 1, :]
            s_t = _dot_nt(k_h, q_h)
            c_prev = c_scr[si]
            n_prev = n_scr[si:si + 1, :]
            carried = _dot_nt(jnp.concatenate([c_prev.astype(BF16), *_split2(n_prev), pad_bf], axis=0), q_h)
            yield
            d_t = jnp.where(src_le_tgt, b_row + w_col, NEG)
            a_row = b_row + m_prev
            m_row = jnp.maximum(a_row, jnp.max(d_t, axis=0, keepdims=True))
            p_t = jnp.exp(d_t - m_row) * s_t
            inter = jnp.exp(a_row - m_row)
            yield
            p_sum = jnp.sum(p_t, axis=0, keepdims=True)
            fresh = _dot(vt_h, p_t.astype(BF16))
            w_row = g - b_row + i_row
            m_new = jnp.maximum(g + m_prev, jnp.max(w_row, axis=1, keepdims=True))
            decay = jnp.exp(g + m_prev - m_new)
            wt = jnp.exp(w_row - m_new)
            upd = _dot(jnp.concatenate([(vt_h.astype(F32) * wt).astype(BF16), *_split2(wt), pad_bf], axis=0), k_h)
            c_scr[si] = decay[:, 0:1] * c_prev + upd[0:hd, :]
            n_scr[si:si + 1, :] = decay[:, 0:1] * n_prev + (upd[hd:hd + 1, :] + upd[hd + 1:hd + 2, :])
            m_scr[si:si + 1, :] = m_new
            yield
            num = inter * carried[0:hd, :] + fresh
            den = inter * (carried[hd:hd + 1, :] + carried[hd + 1:hd + 2, :]) + p_sum
            h_t = num * (1.0 / jnp.maximum(jnp.abs(den), jnp.exp(-m_row)))
            outs[h] = _sigmoid(om_ref[bi, :, hs].astype(F32)) * h_t.T

        for _ in _zip_phases([head(h) for h in range(nh)]):
            yield
        hm = jnp.concatenate(outs, axis=1)
        out_ref[bi] = _rms(hm, gout_ref[...]).astype(BF16)

    for _ in _zip_phases([batch_row(bi) for bi in range(nb)]):
        pass


def _mlstm(l, qk, vt, om, gates, gates_t, gbias, gbias_t, gout):
    nc = SEQ // MLSTM_CHUNK
    nb = MLSTM_BATCH
    blk3 = lambda g, c: (g, c, 0)
    lay = lambda g, c: (l, 0, 0)
    col = lambda bi: (lambda g, c: (0, (g * nb + bi) * nc + c))
    three = lambda t: t.reshape(BATCH, SEQ, t.shape[-1])
    n_chains = nb * N_MLSTM_HEADS
    out = pl.pallas_call(
        _mlstm_kernel,
        grid=(BATCH // nb, nc),
        in_specs=[
            pl.BlockSpec((nb, MLSTM_CHUNK, 2 * MLSTM_WIDTH), blk3),
            pl.BlockSpec((nb, MLSTM_CHUNK, MLSTM_WIDTH), blk3),
            pl.BlockSpec((nb, MLSTM_CHUNK, LANES), blk3),
        ] + [pl.BlockSpec((MLSTM_WIDTH, MLSTM_CHUNK), col(bi)) for bi in range(nb)]
          + [pl.BlockSpec((GATE_ROWS, MLSTM_CHUNK), col(bi)) for bi in range(nb)] + [
            _resident((None, 1, LANES), lay),
            _resident((None, GATE_ROWS, LANES), lay),
            _resident((None, 1, MLSTM_WIDTH), lay),
        ],
        out_specs=pl.BlockSpec((nb, MLSTM_CHUNK, MLSTM_WIDTH), blk3),
        out_shape=jax.ShapeDtypeStruct((BATCH, SEQ, MLSTM_WIDTH), BF16),
        scratch_shapes=[
            pltpu.VMEM((n_chains, MLSTM_HEAD_DIM, MLSTM_HEAD_DIM), F32),
            pltpu.VMEM((n_chains, MLSTM_HEAD_DIM), F32),
            pltpu.VMEM((n_chains, LANES), F32),
        ],
        compiler_params=_params("arbitrary", "arbitrary"),
        name="mlstm",
    )(three(qk), three(om), three(gates), *([vt] * nb), *([gates_t] * nb), gbias, gbias_t, gout)
    return out.reshape(TOKENS, MLSTM_WIDTH)


def _mix_out_kernel(x_ref, attn_ref, hmn_ref, gattn_ref, wout_ref, gpost_ref,
                    gpre_ref, wq_ref, k_ref, v_ref, wo_ref, gpostm_ref, out_ref):
    def sub_tile(r0):
        rows = slice(r0, r0 + SUB_OUT)
        an = _rms(attn_ref[rows, :].astype(F32), gattn_ref[...]).astype(BF16)
        yield
        h = _dot(an, wout_ref[0:ATTN_WIDTH, :]) + _dot(hmn_ref[rows, :], wout_ref[ATTN_WIDTH:, :])
        yield
        x1 = x_ref[rows, :] + _rms(h, gpost_ref[...])
        hq = _rms(x1, gpre_ref[...]).astype(BF16)
        yield
        q = (_dot(hq, wq_ref[...]) * (XHEAD_DIM ** -0.5)).astype(BF16)
        yield
        heads = []
        for hd in range(N_XHEADS):
            sl = slice(hd * XHEAD_DIM, (hd + 1) * XHEAD_DIM)
            logits = _dot_nt(q[:, sl], k_ref[:, sl])
            yield
            m = jnp.max(logits, axis=1, keepdims=True)
            e = jnp.exp(logits - m)
            s = jnp.sum(e, axis=1, keepdims=True)
            yield
            heads.append((_dot(e.astype(BF16), v_ref[:, sl]) / s).astype(BF16))
        o = jnp.concatenate(heads, axis=1)
        yield
        h2 = _dot(o, wo_ref[...])
        yield
        out_ref[rows, :] = x1 + _rms(h2, gpostm_ref[...])

    _interleave([sub_tile(r0) for r0 in range(0, TM_OUT, SUB_OUT)])


def _mix_out(l, x, attn, hmn, gattn, wout, gpost, gpre, wq, k_mem, v_mem, wo, gpostm):
    row = lambda i: (i, 0)
    lay = lambda i: (l, 0, 0)
    tiles_per_seq = SEQ // TM_OUT
    vec = _resident((None, 1, D_MODEL), lay)
    mat = _resident((None, D_MODEL, D_MODEL), lay)
    return pl.pallas_call(
        _mix_out_kernel,
        grid=(TOKENS // TM_OUT,),
        in_specs=[
            pl.BlockSpec((TM_OUT, D_MODEL), row),
            pl.BlockSpec((TM_OUT, ATTN_WIDTH), row),
            pl.BlockSpec((TM_OUT, MLSTM_WIDTH), row),
            _resident((None, 1, ATTN_WIDTH), lay),
            mat, vec, vec, mat,
            pl.BlockSpec((None, N_MEM, D_MODEL), lambda i: (l, i // tiles_per_seq, 0)),
            pl.BlockSpec((None, N_MEM, D_MODEL), lambda i: (l, i // tiles_per_seq, 0)),
            mat, vec,
        ],
        out_specs=pl.BlockSpec((TM_OUT, D_MODEL), row),
        out_shape=jax.ShapeDtypeStruct((TOKENS, D_MODEL), F32),
        compiler_params=_params("parallel"),
        name="mix_out",
    )(x, attn, hmn, gattn, wout, gpost, gpre, wq, k_mem, v_mem, wo, gpostm)


def _ffn_kernel(x_ref, gpre_ref, wup_ref, cw_ref, cb_ref, wdown_ref, gpost_ref, out_ref,
                carry_ref, act_ref, xs_ref, ys_ref):
    assert FFN_CONV == 3

    @pl.when(pl.program_id(0) % (SEQ // TM_FFN) == 0)
    def _():
        carry_ref[...] = jnp.zeros_like(carry_ref)

    n_groups = SUB_FFN // FFN_GROUP
    n_slabs = D_MODEL // LANES
    last_sub = lax.broadcasted_iota(jnp.int32, (SUBLANES, FFN_COLS), 0) == SUBLANES - 1

    for slab, r0 in enumerate(range(0, TM_FFN, SUB_FFN)):
        rows = slice(r0, r0 + SUB_FFN)
        for j in range(n_slabs):
            xs_ref[slab, j] = x_ref[rows, j * LANES:(j + 1) * LANES]
        x = jnp.concatenate([
            jnp.concatenate([xs_ref.at[slab, j][pl.ds(g * FFN_GROUP + v, SUBLANES, stride=SUBLANES), :]
                             for g in range(n_groups) for v in range(SUBLANES)], axis=0)
            for j in range(n_slabs)], axis=1)
        h = _rms(x, gpre_ref[...]).astype(BF16)

        def conv_cols(c0):
            cols = slice(c0, c0 + FFN_COLS)
            u = _dot(h, wup_ref[:, cols])
            vregs = [[u[g * FFN_GROUP + v * SUBLANES:g * FFN_GROUP + (v + 1) * SUBLANES, :]
                      for v in range(SUBLANES)] for g in range(n_groups)]
            prev6 = carry_ref[0:SUBLANES, cols]
            prev7 = carry_ref[SUBLANES:2 * SUBLANES, cols]
            back1, back2 = [], []
            for g in range(n_groups):
                a6 = pltpu.roll(jnp.where(last_sub, prev6, vregs[g][6]), 1, axis=0)
                a7 = pltpu.roll(jnp.where(last_sub, prev7, vregs[g][7]), 1, axis=0)
                back1 += [a7] + vregs[g][0:7]
                back2 += [a6, a7] + vregs[g][0:6]
                prev6, prev7 = vregs[g][6], vregs[g][7]
            carry_ref[0:SUBLANES, cols] = prev6
            carry_ref[SUBLANES:2 * SUBLANES, cols] = prev7
            y = cb_ref[:, cols] + jnp.concatenate(back2, axis=0) * cw_ref[0:1, cols]
            y = y + jnp.concatenate(back1, axis=0) * cw_ref[1:2, cols]
            return y + u * cw_ref[2:3, cols]

        for c in range(D_FF // FFN_COLS):
            a = conv_cols(c * FFN_COLS)
            g = conv_cols(D_FF + c * FFN_COLS)
            gelu = 0.5 * g * (1.0 + jnp.tanh(math.sqrt(2.0 / math.pi) * (g + 0.044715 * (g * g * g))))
            act_ref[rows, c * FFN_COLS:(c + 1) * FFN_COLS] = (gelu * a).astype(BF16)

        out = x + _rms(_dot(act_ref[rows, :], wdown_ref[...]), gpost_ref[...])
        for j in range(n_slabs):
            for g in range(n_groups):
                for v in range(SUBLANES):
                    pr = g * FFN_GROUP + v * SUBLANES
                    dst = pl.ds(g * FFN_GROUP + v, SUBLANES, stride=SUBLANES)
                    ys_ref.at[slab, j][dst, :] = out[pr:pr + SUBLANES, j * LANES:(j + 1) * LANES]
            out_ref[rows, j * LANES:(j + 1) * LANES] = ys_ref[slab, j]


def _ffn(l, x, gpre, wup, cw, cb, wdown, gpost):
    row = lambda i: (i, 0)
    lay = lambda i: (l, 0, 0)
    vec = _resident((None, 1, D_MODEL), lay)
    return pl.pallas_call(
        _ffn_kernel,
        grid=(TOKENS // TM_FFN,),
        in_specs=[
            pl.BlockSpec((TM_FFN, D_MODEL), row),
            vec,
            _resident((None, D_MODEL, 2 * D_FF), lay),
            _resident((None, FFN_CONV, 2 * D_FF), lay),
            _resident((None, 1, 2 * D_FF), lay),
            _resident((None, D_FF, D_MODEL), lay),
            vec,
        ],
        out_specs=pl.BlockSpec((TM_FFN, D_MODEL), row),
        out_shape=jax.ShapeDtypeStruct((TOKENS, D_MODEL), F32),
        scratch_shapes=[
            pltpu.VMEM((2 * SUBLANES, 2 * D_FF), F32),
            pltpu.VMEM((TM_FFN, D_FF), BF16),
            pltpu.VMEM((TM_FFN // SUB_FFN, D_MODEL // LANES, SUB_FFN, LANES), F32),
            pltpu.VMEM((TM_FFN // SUB_FFN, D_MODEL // LANES, SUB_FFN, LANES), F32),
        ],
        compiler_params=_params("arbitrary"),
        name="ffn",
    )(x, gpre, wup, cw, cb, wdown, gpost)


def kernel(x, mem, rel_bias, pre_mix_g, w_in, mconv_w, mconv_b, b_igate, b_fgate, attn_out_g,
           mlstm_out_g, w_out, post_mix_g, pre_mem_g, wq_mem, wk_mem, wv_mem, wo_mem, post_mem_g,
           pre_ffn_g, w_up, fconv_w, fconv_b, w_down, post_ffn_g):
    vec3 = lambda t: t.reshape(DEPTH, 1, -1)

    v0 = 3 * ATTN_WIDTH + 2 * MLSTM_WIDTH
    g0 = v0 + 2 * MLSTM_WIDTH
    col_scale = jnp.where(jnp.arange(v0) < ATTN_WIDTH, ATTN_HEAD_DIM ** -0.5, 1.0).astype(F32)
    w_main = (w_in[:, :, :v0] * col_scale).astype(BF16)
    w_om = w_in[:, :, v0 + MLSTM_WIDTH:g0].astype(BF16)
    w_vt = jnp.swapaxes(w_in[:, :, v0:v0 + MLSTM_WIDTH], 1, 2).astype(BF16)
    w_gate = jnp.pad(w_in[:, :, g0:], ((0, 0), (0, 0), (0, LANES - GATE_ROWS))).astype(BF16)
    w_gate_t = jnp.pad(jnp.swapaxes(w_in[:, :, g0:], 1, 2), ((0, 0), (0, GATE_ROWS), (0, 0))).astype(BF16)
    gate_bias = jnp.concatenate([b_igate, b_fgate], axis=1)
    gbias = jnp.pad(gate_bias, ((0, 0), (0, LANES - GATE_ROWS)))
    gbias_t = jnp.broadcast_to(gate_bias[:, :, None], (DEPTH, GATE_ROWS, LANES))
    ksc = jnp.concatenate([jnp.ones((1, MLSTM_WIDTH), F32),
                           jnp.full((1, MLSTM_WIDTH), MLSTM_HEAD_DIM ** -0.5, F32)], axis=1)
    w_out_bf = w_out.astype(BF16)
    wq_bf = wq_mem.astype(BF16)
    wo_bf = wo_mem.astype(BF16)
    w_up_bf = w_up.astype(BF16)
    w_down_bf = w_down.astype(BF16)
    bias = _attn_bias_table(rel_bias)

    k_mem, v_mem = _memkv(mem.reshape(BATCH * N_MEM, D_MODEL).astype(BF16), wk_mem, wv_mem)

    xs = x.reshape(TOKENS, D_MODEL)
    for l in range(DEPTH):
        qkv, qk_wide, qk, vt, om, gates, gates_t = _mix_in(
            l, xs, vec3(pre_mix_g), w_main, w_vt, w_om, w_gate, w_gate_t, mconv_w, vec3(mconv_b), ksc)
        attn = _dil_attn(qkv, qk_wide, bias)
        hmn = _mlstm(l, qk, vt, om, gates, gates_t, vec3(gbias), gbias_t, vec3(mlstm_out_g))
        xs = _mix_out(l, xs, attn, hmn, vec3(attn_out_g), w_out_bf, vec3(post_mix_g),
                      vec3(pre_mem_g), wq_bf, k_mem, v_mem, wo_bf, vec3(post_mem_g))
        xs = _ffn(l, xs, vec3(pre_ffn_g), w_up_bf, fconv_w, vec3(fconv_b), w_down_bf,
                  vec3(post_ffn_g))
    return xs.reshape(BATCH, SEQ, D_MODEL)
```

```python
import functools
import math

import jax
import jax.numpy as jnp
import numpy as np
from jax import lax
from jax.experimental import pallas as pl
from jax.experimental.pallas import tpu as pltpu

F32 = jnp.float32
BF16 = jnp.bfloat16

D_MODEL = 1024
BATCH = 4
SEQ = 4096
DEPTH = 4
TOKENS = BATCH * SEQ

ATTN_HEAD_DIM = 64
ATTN_WIDTH = 512
N_ATTN_HEADS = 8
DILATIONS = (1, 4, 16)
ATTN_BLK = 128
N_MLSTM_HEADS = 4
MLSTM_WIDTH = 512
MLSTM_HEAD_DIM = 128
MLSTM_CHUNK = 128
MLSTM_CONV = 4
N_BUCKETS = 32
MAX_DISTANCE = 2048
N_MEM = 256
N_XHEADS = 4
XHEAD_DIM = 256
D_FF = 2816
FFN_CONV = 3
EPS = 1e-6

LANES = 128
SUBLANES = 8
NEG = -math.inf
VMEM_LIMIT = 56 * 1024 * 1024

TM_IN = 1024
SUB_IN = 256
MIXIN_COLS = 256
TM_OUT = 1024
SUB_OUT = 256
TM_FFN = 512
SUB_FFN = 256
FFN_GROUP = SUBLANES * SUBLANES
FFN_COLS = 256
MAIN_COLS = 3 * ATTN_WIDTH + 2 * MLSTM_WIDTH
GATE_ROWS = 2 * N_MLSTM_HEADS
MLSTM_BATCH = BATCH
MLSTM_OPERAND_ROWS = MLSTM_HEAD_DIM + 16

NT = (((1,), (1,)), ((), ()))


def _dot(a, b):
    return jnp.dot(a, b, preferred_element_type=F32)


def _dot_nt(a, b):
    return lax.dot_general(a, b, NT, preferred_element_type=F32)


def _rms(x, g):
    return x * lax.rsqrt(jnp.mean(x * x, axis=-1, keepdims=True) + EPS) * g


def _sigmoid(x):
    return 1.0 / (1.0 + jnp.exp(-x))


def _interleave(gens):
    pending = list(gens)
    active = []
    while pending or active:
        if pending:
            active.append(pending.pop(0))
        for g in list(active):
            try:
                next(g)
            except StopIteration:
                active.remove(g)


def _resident(block_shape, index_map):
    return pl.BlockSpec(block_shape, index_map, pipeline_mode=pl.Buffered(1))


def _params(*semantics):
    return pltpu.CompilerParams(dimension_semantics=semantics, vmem_limit_bytes=VMEM_LIMIT)


def _memkv_kernel(mem_ref, wk_ref, wv_ref, k_ref, v_ref):
    mem = mem_ref[...]
    k_ref[...] = _dot(mem, wk_ref[...].astype(BF16)).astype(BF16)
    v_ref[...] = _dot(mem, wv_ref[...].astype(BF16)).astype(BF16)


def _memkv(mem_bf, wk, wv):
    rows = BATCH * N_MEM
    w_spec = pl.BlockSpec((None, D_MODEL, D_MODEL), lambda l: (l, 0, 0))
    o_spec = pl.BlockSpec((None, rows, D_MODEL), lambda l: (l, 0, 0))
    o_shape = jax.ShapeDtypeStruct((DEPTH, rows, D_MODEL), BF16)
    return pl.pallas_call(
        _memkv_kernel,
        grid=(DEPTH,),
        in_specs=[_resident((rows, D_MODEL), lambda l: (0, 0)), w_spec, w_spec],
        out_specs=[o_spec, o_spec],
        out_shape=[o_shape, o_shape],
        compiler_params=_params("parallel"),
        name="memkv",
    )(mem_bf, wk, wv)


def _mix_in_kernel(x_ref, g_ref, w_ref, wvt_ref, wom_ref, wg_ref, wgt_ref, cw_ref, cb_ref, ksc_ref,
                   qkv_ref, qkwide_ref, qk_ref, vt_ref, om_ref, gate_ref, gatet_ref, buf_ref):
    @pl.when(pl.program_id(0) % (SEQ // TM_IN) == 0)
    def _():
        buf_ref[0:SUBLANES, :] = jnp.zeros((SUBLANES, 2 * MLSTM_WIDTH), F32)

    a0 = 3 * ATTN_WIDTH
    cw_cols = MIXIN_COLS
    n_conv = 2 * MLSTM_WIDTH // cw_cols

    def sub_tile(r0):
        rows = slice(r0, r0 + SUB_IN)
        h = _rms(x_ref[rows, :], g_ref[...]).astype(BF16)
        yield

        def plain(out_ref, wsrc_ref, c0):
            res = _dot(h, wsrc_ref[:, c0:c0 + cw_cols])
            if out_ref is qkv_ref and c0 < ATTN_WIDTH:
                res = res * (ATTN_HEAD_DIM ** -0.5)
            res = res.astype(BF16)
            out_ref[rows, c0:c0 + cw_cols] = res
            if out_ref is qkv_ref and c0 < 2 * ATTN_WIDTH:
                qkwide_ref[rows, c0:c0 + cw_cols] = res.astype(F32)

        def v_transposed(f0, _):
            vt_ref[f0:f0 + cw_cols, rows] = _dot_nt(wvt_ref[f0:f0 + cw_cols, :], h).astype(BF16)

        plain_jobs = ([(plain, qkv_ref, w_ref, c) for c in range(0, a0, cw_cols)]
                      + [(v_transposed, c, None) for c in range(0, MLSTM_WIDTH, cw_cols)]
                      + [(plain, om_ref, wom_ref, c) for c in range(0, MLSTM_WIDTH, cw_cols)])
        per_conv = -(-len(plain_jobs) // n_conv)
        for cc in range(n_conv):
            cols = slice(cc * cw_cols, (cc + 1) * cw_cols)
            pre = _dot(h, w_ref[:, a0 + cc * cw_cols:a0 + (cc + 1) * cw_cols])
            buf_ref[SUBLANES + r0:SUBLANES + r0 + SUB_IN, cols] = pre
            yield
            for fn, *job in plain_jobs[cc * per_conv:(cc + 1) * per_conv]:
                fn(*job)
            y = cb_ref[:, cols]
            for j in range(MLSTM_CONV - 1):
                lo = SUBLANES + r0 - (MLSTM_CONV - 1) + j
                y = y + buf_ref[lo:lo + SUB_IN, cols] * cw_ref[j:j + 1, cols]
            y = y + pre * cw_ref[MLSTM_CONV - 1:MLSTM_CONV, cols]
            qk_ref[rows, cols] = (y * _sigmoid(y) * ksc_ref[:, cols]).astype(BF16)
            yield
        gate_ref[rows, :] = _dot(h, wg_ref[...])
        gatet_ref[:, rows] = _dot_nt(wgt_ref[...], h)[0:GATE_ROWS, :]

    _interleave([sub_tile(r0) for r0 in range(0, TM_IN, SUB_IN)])
    buf_ref[0:SUBLANES, :] = buf_ref[TM_IN:TM_IN + SUBLANES, :]


def _mix_in(l, x, pre_g, w_main, w_vt, w_om, w_gate, w_gate_t, cw, cb, ksc):
    row = lambda i: (i, 0)
    col = lambda i: (0, i)
    lay = lambda i: (l, 0, 0)
    return pl.pallas_call(
        _mix_in_kernel,
        grid=(TOKENS // TM_IN,),
        in_specs=[
            pl.BlockSpec((TM_IN, D_MODEL), row),
            _resident((None, 1, D_MODEL), lay),
            _resident((None, D_MODEL, MAIN_COLS), lay),
            _resident((None, MLSTM_WIDTH, D_MODEL), lay),
            _resident((None, D_MODEL, MLSTM_WIDTH), lay),
            _resident((None, D_MODEL, LANES), lay),
            _resident((None, 2 * GATE_ROWS, D_MODEL), lay),
            _resident((None, MLSTM_CONV, 2 * MLSTM_WIDTH), lay),
            _resident((None, 1, 2 * MLSTM_WIDTH), lay),
            _resident((1, 2 * MLSTM_WIDTH), lambda i: (0, 0)),
        ],
        out_specs=[
            pl.BlockSpec((TM_IN, 3 * ATTN_WIDTH), row),
            pl.BlockSpec((TM_IN, 2 * ATTN_WIDTH), row),
            pl.BlockSpec((TM_IN, 2 * MLSTM_WIDTH), row),
            pl.BlockSpec((MLSTM_WIDTH, TM_IN), col),
            pl.BlockSpec((TM_IN, MLSTM_WIDTH), row),
            pl.BlockSpec((TM_IN, LANES), row),
            pl.BlockSpec((GATE_ROWS, TM_IN), col),
        ],
        out_shape=[
            jax.ShapeDtypeStruct((TOKENS, 3 * ATTN_WIDTH), BF16),
            jax.ShapeDtypeStruct((TOKENS, 2 * ATTN_WIDTH), F32),
            jax.ShapeDtypeStruct((TOKENS, 2 * MLSTM_WIDTH), BF16),
            jax.ShapeDtypeStruct((MLSTM_WIDTH, TOKENS), BF16),
            jax.ShapeDtypeStruct((TOKENS, MLSTM_WIDTH), BF16),
            jax.ShapeDtypeStruct((TOKENS, LANES), F32),
            jax.ShapeDtypeStruct((GATE_ROWS, TOKENS), F32),
        ],
        scratch_shapes=[pltpu.VMEM((TM_IN + SUBLANES, 2 * MLSTM_WIDTH), F32)],
        compiler_params=_params("arbitrary"),
        name="mix_in",
    )(x, pre_g, w_main, w_vt, w_om, w_gate, w_gate_t, cw, cb, ksc)


N_UNITS = SEQ // ATTN_BLK
PIPE_DEPTH = 4
ONES_ROWS = 16


def _div(u, n):
    return lax.shift_right_logical(u, int(math.log2(n)))


def _mod(u, n):
    return lax.bitwise_and(u, n - 1)


def _dil_attn_kernel(q_ref, k_ref, v_ref, qf, kf, bias_ref, out_ref,
                     vf, q4f, k4f, v4f, dq, dk, dvt, o_s0, o_s1, o_s2, l_s0, l_s1, l_s2):
    d1, d4, d16 = DILATIONS
    hd = ATTN_HEAD_DIM
    dvt[LANES:LANES + ONES_ROWS, :] = jnp.ones((ONES_ROWS, SEQ), BF16)

    def widen(r, carry):
        rows = pl.ds(pl.multiple_of(r * ATTN_BLK, ATTN_BLK), ATTN_BLK)
        v = v_ref[rows, :].astype(F32)
        vf[rows, :] = v
        dvt[0:LANES, rows] = v.T.astype(BF16)
        return carry

    lax.fori_loop(0, N_UNITS, widen, 0, unroll=8)

    lane = lax.broadcasted_iota(jnp.int32, (ATTN_BLK, LANES), 1)
    first_head = lane < hd
    key_in_prev = lax.broadcasted_iota(jnp.int32, (2 * ATTN_BLK, 2 * ATTN_BLK), 0) < ATTN_BLK

    def deinterleave(src_refs, keep_refs, stride, start_of):
        def body(u, carry):
            src = pl.ds(start_of(u), ATTN_BLK, stride=stride)
            dst = pl.ds(pl.multiple_of(u * ATTN_BLK, ATTN_BLK), ATTN_BLK)
            q, k, v = (r[src, :] for r in src_refs)
            if keep_refs is not None:
                for r, val in zip(keep_refs, (q, k, v)):
                    r[dst, :] = val
            dq[dst, :] = q.astype(BF16)
            dk[dst, :] = k.astype(BF16)
            dvt[0:LANES, dst] = v.T.astype(BF16)
            return carry
        lax.fori_loop(0, N_UNITS, body, 0, unroll=8)

    def run_pattern(p, d, q_src, k_src, o_scr, l_scr):
        nb = N_UNITS // d

        def qk(u):
            cur = pl.ds(u * ATTN_BLK, ATTN_BLK)
            prev = pl.ds(max(u - 1, 0) * ATTN_BLK, ATTN_BLK)
            q = q_src[cur, :]
            zero = jnp.zeros_like(q)
            q2 = jnp.concatenate([jnp.where(first_head, q, zero), jnp.where(first_head, zero, q)], axis=0)
            kwin = jnp.concatenate([k_src[prev, :], k_src[cur, :]], axis=0)
            bias = bias_ref[p]
            if u % nb == 0:
                bias = jnp.where(key_in_prev, NEG, bias)
            return _dot_nt(kwin, q2) + bias

        def rest(u, st):
            cur = pl.ds(u * ATTN_BLK, ATTN_BLK)
            prev = pl.ds(max(u - 1, 0) * ATTN_BLK, ATTN_BLK)
            m = jnp.max(st, axis=0, keepdims=True)
            e = jnp.exp(st - m)
            vtwin = jnp.concatenate([dvt[:, prev], dvt[:, cur]], axis=1)
            ot = _dot(vtwin, e.astype(BF16))
            s = ot[LANES:LANES + 1, :]
            inv = 1.0 / s
            lse = m + jnp.log(s)
            o_t = jnp.concatenate([ot[0:hd, 0:ATTN_BLK] * inv[:, 0:ATTN_BLK],
                                   ot[hd:2 * hd, ATTN_BLK:] * inv[:, ATTN_BLK:]], axis=0)
            l_t = jnp.concatenate([jnp.broadcast_to(lse[:, 0:ATTN_BLK], (hd, ATTN_BLK)),
                                   jnp.broadcast_to(lse[:, ATTN_BLK:], (hd, ATTN_BLK))], axis=0)
            dst = pl.ds(u // nb + (u % nb) * (ATTN_BLK * d), ATTN_BLK, stride=d)
            o_scr[dst, :] = o_t.T
            l_scr[dst, :] = l_t.T

        pending = {}
        for u in range(N_UNITS + PIPE_DEPTH):
            if u < N_UNITS:
                pending[u] = qk(u)
            if u >= PIPE_DEPTH:
                rest(u - PIPE_DEPTH, pending.pop(u - PIPE_DEPTH))

    run_pattern(0, d1, q_ref, k_ref, o_s0, l_s0)
    per4 = SEQ // d4
    nb4 = N_UNITS // d4
    deinterleave((qf, kf, vf), (q4f, k4f, v4f), d4,
                 lambda u: _div(u, nb4) + _mod(u, nb4) * (ATTN_BLK * d4))
    run_pattern(1, d4, dq, dk, o_s1, l_s1)
    ratio = d16 // d4
    nb16 = N_UNITS // d16
    deinterleave((q4f, k4f, v4f), None, ratio,
                 lambda u: (_mod(_div(u, nb16), d4) * per4 + _mod(u, nb16) * (ATTN_BLK * ratio)
                            + _div(_div(u, nb16), d4)))
    run_pattern(2, d16, dq, dk, o_s2, l_s2)

    def combine(r, carry):
        rows = pl.ds(pl.multiple_of(r * ATTN_BLK, ATTN_BLK), ATTN_BLK)
        la, lb, lc = l_s0[rows, :], l_s1[rows, :], l_s2[rows, :]
        mx = jnp.maximum(jnp.maximum(la, lb), lc)
        ea, eb, ec = jnp.exp(la - mx), jnp.exp(lb - mx), jnp.exp(lc - mx)
        mixed = (ea * o_s0[rows, :] + eb * o_s1[rows, :] + ec * o_s2[rows, :]) / (ea + eb + ec)
        out_ref[rows, :] = mixed.astype(BF16)
        return carry

    lax.fori_loop(0, N_UNITS, combine, 0, unroll=2)


def _dil_attn(qkv, qk_wide, bias):
    pairs = N_ATTN_HEADS // 2
    blk = (SEQ, LANES)
    f32_seq = pltpu.VMEM((SEQ, LANES), F32)
    bf_seq = pltpu.VMEM((SEQ, LANES), BF16)
    return pl.pallas_call(
        _dil_attn_kernel,
        grid=(BATCH, pairs),
        in_specs=[
            pl.BlockSpec(blk, lambda b, hp: (b, hp)),
            pl.BlockSpec(blk, lambda b, hp: (b, pairs + hp)),
            pl.BlockSpec(blk, lambda b, hp: (b, 2 * pairs + hp)),
            pl.BlockSpec(blk, lambda b, hp: (b, hp)),
            pl.BlockSpec(blk, lambda b, hp: (b, pairs + hp)),
            pl.BlockSpec((None, len(DILATIONS), 2 * ATTN_BLK, 2 * ATTN_BLK),
                         lambda b, hp: (hp, 0, 0, 0)),
        ],
        out_specs=pl.BlockSpec(blk, lambda b, hp: (b, hp)),
        out_shape=jax.ShapeDtypeStruct((TOKENS, ATTN_WIDTH), BF16),
        scratch_shapes=([f32_seq] * 4 + [bf_seq] * 2 + [pltpu.VMEM((LANES + ONES_ROWS, SEQ), BF16)]
                        + [f32_seq] * 6),
        compiler_params=_params("parallel", "parallel"),
        name="dil_attn",
    )(qkv, qkv, qkv, qk_wide, qk_wide, bias)


def _t5_causal_bucket(dist):
    max_exact = N_BUCKETS // 2
    d_f = jnp.maximum(dist, 1).astype(F32)
    large = max_exact + (jnp.log(d_f / max_exact) / math.log(MAX_DISTANCE / max_exact)
                         * (N_BUCKETS - max_exact)).astype(jnp.int32)
    large = jnp.minimum(large, N_BUCKETS - 1)
    return jnp.where(dist < max_exact, dist, large)


def _attn_bias_table(rel_bias):
    pairs = N_ATTN_HEADS // 2
    n_pat = len(DILATIONS)
    t = _attn_bias_qk(rel_bias).reshape(n_pat, pairs, 2, ATTN_BLK, 2 * ATTN_BLK)
    return jnp.transpose(t, (1, 0, 4, 2, 3)).reshape(pairs, n_pat, 2 * ATTN_BLK, 2 * ATTN_BLK)


def _attn_bias_qk(rel_bias):
    blk = ATTN_BLK
    period = 3 * blk + 1
    pad_lo = jnp.full((N_ATTN_HEADS, blk - 1), NEG, F32)
    pad_hi = jnp.full((N_ATTN_HEADS, period - 2 * blk), NEG, F32)
    tables = []
    for d in DILATIONS:
        bias_sub = rel_bias[_t5_causal_bucket(jnp.arange(blk + 1) * d)].T.astype(F32)
        r = jnp.concatenate([pad_lo, bias_sub[:, ::-1], pad_hi], axis=1)
        tiled = jnp.broadcast_to(r[:, None, :], (N_ATTN_HEADS, blk, period))
        skew = tiled.reshape(N_ATTN_HEADS, blk * period)[:, :blk * (period - 1)]
        skew = skew.reshape(N_ATTN_HEADS, blk, period - 1)
        tables.append(skew[:, :, blk - 1:3 * blk - 1])
    return jnp.stack(tables)


def _log_sigmoid(x):
    return jnp.minimum(x, 0.0) - jnp.log1p(jnp.exp(-jnp.abs(x)))


def _split2(x):
    hi = x.astype(BF16)
    return hi, (x - hi.astype(F32)).astype(BF16)


def _split3(x):
    hi, mid = _split2(x)
    lo = (x - hi.astype(F32) - mid.astype(F32)).astype(BF16)
    return hi, mid, lo


def _zip_phases(gens):
    pending = list(gens)
    active = []
    while pending or active:
        if pending:
            active.append(pending.pop(0))
        for g in list(active):
            try:
                next(g)
            except StopIteration:
                active.remove(g)
        yield


def _mlstm_kernel(*refs):
    nb = MLSTM_BATCH
    qk_ref, om_ref, gate_ref = refs[0:3]
    vt_refs = refs[3:3 + nb]
    gatet_refs = refs[3 + nb:3 + 2 * nb]
    gbias_ref, gbiast_ref, gout_ref, out_ref, c_scr, n_scr, m_scr = refs[3 + 2 * nb:]

    @pl.when(pl.program_id(1) == 0)
    def _():
        c_scr[...] = jnp.zeros_like(c_scr)
        n_scr[...] = jnp.zeros_like(n_scr)
        m_scr[...] = jnp.zeros_like(m_scr)

    nh = N_MLSTM_HEADS
    ch = MLSTM_CHUNK
    hd = MLSTM_HEAD_DIM
    row = lax.broadcasted_iota(jnp.int32, (ch, LANES), 0)
    lane = lax.broadcasted_iota(jnp.int32, (ch, LANES), 1)
    src_le_tgt = row <= lane
    tri_lower = (row >= lane).astype(BF16)
    tri_upper = src_le_tgt.astype(BF16)
    grow = lax.broadcasted_iota(jnp.int32, (GATE_ROWS, LANES), 0)
    pad_bf = jnp.zeros((MLSTM_OPERAND_ROWS - hd - 2, ch), BF16)

    def batch_row(bi):
        gates = gate_ref[bi] + gbias_ref[...]
        logf = jnp.where((lane >= nh) & (lane < 2 * nh), _log_sigmoid(gates), 0.0)
        hi, mid, lo = _split3(logf)
        bcum = _dot(tri_lower, hi) + _dot(tri_lower, mid) + _dot(tri_lower, lo)
        gates_t = gatet_refs[bi][...] + gbiast_ref[...]
        logf_t = jnp.where(grow >= nh, _log_sigmoid(gates_t), 0.0)
        hi, mid, lo = _split3(logf_t)
        bcum_t = _dot(hi, tri_upper) + _dot(mid, tri_upper) + _dot(lo, tri_upper)
        yield
        outs = [None] * nh

        def head(h):
            si = bi * nh + h
            hs = slice(h * hd, (h + 1) * hd)
            ks = slice(MLSTM_WIDTH + h * hd, MLSTM_WIDTH + (h + 1) * hd)
            q_h = qk_ref[bi, :, hs]
            k_h = qk_ref[bi, :, ks]
            vt_h = vt_refs[bi][hs, :]
            b_row = bcum_t[nh + h:nh + h + 1, :]
            i_row = gates_t[h:h + 1, :]
            w_col = gates[:, h:h + 1] - bcum[:, nh + h:nh + h + 1]
            g = b_row[:, ch - 1:ch]
            m_prev = m_scr[si:si + 1, :]
            s_t = _dot_nt(k_h, q_h)
            c_prev = c_scr[si]
            n_prev = n_scr[si:si + 1, :]
            carried = _dot_nt(jnp.concatenate([c_prev.astype(BF16), *_split2(n_prev), pad_bf], axis=0), q_h)
            yield
            d_t = jnp.where(src_le_tgt, b_row + w_col, NEG)
            a_row = b_row + m_prev
            m_row = jnp.maximum(a_row, jnp.max(d_t, axis=0, keepdims=True))
            p_t = jnp.exp(d_t - m_row) * s_t
            inter = jnp.exp(a_row - m_row)
            yield
            p_sum = jnp.sum(p_t, axis=0, keepdims=True)
            fresh = _dot(vt_h, p_t.astype(BF16))
            w_row = g - b_row + i_row
            m_new = jnp.maximum(g + m_prev, jnp.max(w_row, axis=1, keepdims=True))
            decay = jnp.exp(g + m_prev - m_new)
            wt = jnp.exp(w_row - m_new)
            upd = _dot(jnp.concatenate([(vt_h.astype(F32) * wt).astype(BF16), *_split2(wt), pad_bf], axis=0), k_h)
            c_scr[si] = decay[:, 0:1] * c_prev + upd[0:hd, :]
            n_scr[si:si + 1, :] = decay[:, 0:1] * n_prev + (upd[hd:hd + 1, :] + upd[hd + 1:hd + 2, :])
            m_scr[si:si + 1, :] = m_new
            yield
            num = inter * carried[0:hd, :] + fresh
            den = inter * (carried[hd:hd + 1, :] + carried[hd + 1:hd + 2, :]) + p_sum
            h_t = num * (1.0 / jnp.maximum(jnp.abs(den), jnp.exp(-m_row)))
            outs[h] = _sigmoid(om_ref[bi, :, hs].astype(F32)) * h_t.T

        for _ in _zip_phases([head(h) for h in range(nh)]):
            yield
        hm = jnp.concatenate(outs, axis=1)
        out_ref[bi] = _rms(hm, gout_ref[...]).astype(BF16)

    for _ in _zip_phases([batch_row(bi) for bi in range(nb)]):
        pass


def _mlstm(l, qk, vt, om, gates, gates_t, gbias, gbias_t, gout):
    nc = SEQ // MLSTM_CHUNK
    nb = MLSTM_BATCH
    blk3 = lambda g, c: (g, c, 0)
    lay = lambda g, c: (l, 0, 0)
    col = lambda bi: (lambda g, c: (0, (g * nb + bi) * nc + c))
    three = lambda t: t.reshape(BATCH, SEQ, t.shape[-1])
    n_chains = nb * N_MLSTM_HEADS
    out = pl.pallas_call(
        _mlstm_kernel,
        grid=(BATCH // nb, nc),
        in_specs=[
            pl.BlockSpec((nb, MLSTM_CHUNK, 2 * MLSTM_WIDTH), blk3),
            pl.BlockSpec((nb, MLSTM_CHUNK, MLSTM_WIDTH), blk3),
            pl.BlockSpec((nb, MLSTM_CHUNK, LANES), blk3),
        ] + [pl.BlockSpec((MLSTM_WIDTH, MLSTM_CHUNK), col(bi)) for bi in range(nb)]
          + [pl.BlockSpec((GATE_ROWS, MLSTM_CHUNK), col(bi)) for bi in range(nb)] + [
            _resident((None, 1, LANES), lay),
            _resident((None, GATE_ROWS, LANES), lay),
            _resident((None, 1, MLSTM_WIDTH), lay),
        ],
        out_specs=pl.BlockSpec((nb, MLSTM_CHUNK, MLSTM_WIDTH), blk3),
        out_shape=jax.ShapeDtypeStruct((BATCH, SEQ, MLSTM_WIDTH), BF16),
        scratch_shapes=[
            pltpu.VMEM((n_chains, MLSTM_HEAD_DIM, MLSTM_HEAD_DIM), F32),
            pltpu.VMEM((n_chains, MLSTM_HEAD_DIM), F32),
            pltpu.VMEM((n_chains, LANES), F32),
        ],
        compiler_params=_params("arbitrary", "arbitrary"),
        name="mlstm",
    )(three(qk), three(om), three(gates), *([vt] * nb), *([gates_t] * nb), gbias, gbias_t, gout)
    return out.reshape(TOKENS, MLSTM_WIDTH)


def _mix_out_kernel(x_ref, attn_ref, hmn_ref, gattn_ref, wout_ref, gpost_ref,
                    gpre_ref, wq_ref, k_ref, v_ref, wo_ref, gpostm_ref, out_ref):
    def sub_tile(r0):
        rows = slice(r0, r0 + SUB_OUT)
        an = _rms(attn_ref[rows, :].astype(F32), gattn_ref[...]).astype(BF16)
        yield
        h = _dot(an, wout_ref[0:ATTN_WIDTH, :]) + _dot(hmn_ref[rows, :], wout_ref[ATTN_WIDTH:, :])
        yield
        x1 = x_ref[rows, :] + _rms(h, gpost_ref[...])
        hq = _rms(x1, gpre_ref[...]).astype(BF16)
        yield
        q = (_dot(hq, wq_ref[...]) * (XHEAD_DIM ** -0.5)).astype(BF16)
        yield
        heads = []
        for hd in range(N_XHEADS):
            sl = slice(hd * XHEAD_DIM, (hd + 1) * XHEAD_DIM)
            logits = _dot_nt(q[:, sl], k_ref[:, sl])
            yield
            m = jnp.max(logits, axis=1, keepdims=True)
            e = jnp.exp(logits - m)
            s = jnp.sum(e, axis=1, keepdims=True)
            yield
            heads.append((_dot(e.astype(BF16), v_ref[:, sl]) / s).astype(BF16))
        o = jnp.concatenate(heads, axis=1)
        yield
        h2 = _dot(o, wo_ref[...])
        yield
        out_ref[rows, :] = x1 + _rms(h2, gpostm_ref[...])

    _interleave([sub_tile(r0) for r0 in range(0, TM_OUT, SUB_OUT)])


def _mix_out(l, x, attn, hmn, gattn, wout, gpost, gpre, wq, k_mem, v_mem, wo, gpostm):
    row = lambda i: (i, 0)
    lay = lambda i: (l, 0, 0)
    tiles_per_seq = SEQ // TM_OUT
    vec = _resident((None, 1, D_MODEL), lay)
    mat = _resident((None, D_MODEL, D_MODEL), lay)
    return pl.pallas_call(
        _mix_out_kernel,
        grid=(TOKENS // TM_OUT,),
        in_specs=[
            pl.BlockSpec((TM_OUT, D_MODEL), row),
            pl.BlockSpec((TM_OUT, ATTN_WIDTH), row),
            pl.BlockSpec((TM_OUT, MLSTM_WIDTH), row),
            _resident((None, 1, ATTN_WIDTH), lay),
            mat, vec, vec, mat,
            pl.BlockSpec((None, N_MEM, D_MODEL), lambda i: (l, i // tiles_per_seq, 0)),
            pl.BlockSpec((None, N_MEM, D_MODEL), lambda i: (l, i // tiles_per_seq, 0)),
            mat, vec,
        ],
        out_specs=pl.BlockSpec((TM_OUT, D_MODEL), row),
        out_shape=jax.ShapeDtypeStruct((TOKENS, D_MODEL), F32),
        compiler_params=_params("parallel"),
        name="mix_out",
    )(x, attn, hmn, gattn, wout, gpost, gpre, wq, k_mem, v_mem, wo, gpostm)


def _ffn_kernel(x_ref, gpre_ref, wup_ref, cw_ref, cb_ref, wdown_ref, gpost_ref, out_ref,
                carry_ref, act_ref, xs_ref, ys_ref):
    assert FFN_CONV == 3

    @pl.when(pl.program_id(0) % (SEQ // TM_FFN) == 0)
    def _():
        carry_ref[...] = jnp.zeros_like(carry_ref)

    n_groups = SUB_FFN // FFN_GROUP
    n_slabs = D_MODEL // LANES
    last_sub = lax.broadcasted_iota(jnp.int32, (SUBLANES, FFN_COLS), 0) == SUBLANES - 1

    for slab, r0 in enumerate(range(0, TM_FFN, SUB_FFN)):
        rows = slice(r0, r0 + SUB_FFN)
        for j in range(n_slabs):
            xs_ref[slab, j] = x_ref[rows, j * LANES:(j + 1) * LANES]
        x = jnp.concatenate([
            jnp.concatenate([xs_ref.at[slab, j][pl.ds(g * FFN_GROUP + v, SUBLANES, stride=SUBLANES), :]
                             for g in range(n_groups) for v in range(SUBLANES)], axis=0)
            for j in range(n_slabs)], axis=1)
        h = _rms(x, gpre_ref[...]).astype(BF16)

        def conv_cols(c0):
            cols = slice(c0, c0 + FFN_COLS)
            u = _dot(h, wup_ref[:, cols])
            vregs = [[u[g * FFN_GROUP + v * SUBLANES:g * FFN_GROUP + (v + 1) * SUBLANES, :]
                      for v in range(SUBLANES)] for g in range(n_groups)]
            prev6 = carry_ref[0:SUBLANES, cols]
            prev7 = carry_ref[SUBLANES:2 * SUBLANES, cols]
            back1, back2 = [], []
            for g in range(n_groups):
                a6 = pltpu.roll(jnp.where(last_sub, prev6, vregs[g][6]), 1, axis=0)
                a7 = pltpu.roll(jnp.where(last_sub, prev7, vregs[g][7]), 1, axis=0)
                back1 += [a7] + vregs[g][0:7]
                back2 += [a6, a7] + vregs[g][0:6]
                prev6, prev7 = vregs[g][6], vregs[g][7]
            carry_ref[0:SUBLANES, cols] = prev6
            carry_ref[SUBLANES:2 * SUBLANES, cols] = prev7
            y = cb_ref[:, cols] + jnp.concatenate(back2, axis=0) * cw_ref[0:1, cols]
            y = y + jnp.concatenate(back1, axis=0) * cw_ref[1:2, cols]
            return y + u * cw_ref[2:3, cols]

        for c in range(D_FF // FFN_COLS):
            a = conv_cols(c * FFN_COLS)
            g = conv_cols(D_FF + c * FFN_COLS)
            gelu = 0.5 * g * (1.0 + jnp.tanh(math.sqrt(2.0 / math.pi) * (g + 0.044715 * (g * g * g))))
            act_ref[rows, c * FFN_COLS:(c + 1) * FFN_COLS] = (gelu * a).astype(BF16)

        out = x + _rms(_dot(act_ref[rows, :], wdown_ref[...]), gpost_ref[...])
        for j in range(n_slabs):
            for g in range(n_groups):
                for v in range(SUBLANES):
                    pr = g * FFN_GROUP + v * SUBLANES
                    dst = pl.ds(g * FFN_GROUP + v, SUBLANES, stride=SUBLANES)
                    ys_ref.at[slab, j][dst, :] = out[pr:pr + SUBLANES, j * LANES:(j + 1) * LANES]
            out_ref[rows, j * LANES:(j + 1) * LANES] = ys_ref[slab, j]


def _ffn(l, x, gpre, wup, cw, cb, wdown, gpost):
    row = lambda i: (i, 0)
    lay = lambda i: (l, 0, 0)
    vec = _resident((None, 1, D_MODEL), lay)
    return pl.pallas_call(
        _ffn_kernel,
        grid=(TOKENS // TM_FFN,),
        in_specs=[
            pl.BlockSpec((TM_FFN, D_MODEL), row),
            vec,
            _resident((None, D_MODEL, 2 * D_FF), lay),
            _resident((None, FFN_CONV, 2 * D_FF), lay),
            _resident((None, 1, 2 * D_FF), lay),
            _resident((None, D_FF, D_MODEL), lay),
            vec,
        ],
        out_specs=pl.BlockSpec((TM_FFN, D_MODEL), row),
        out_shape=jax.ShapeDtypeStruct((TOKENS, D_MODEL), F32),
        scratch_shapes=[
            pltpu.VMEM((2 * SUBLANES, 2 * D_FF), F32),
            pltpu.VMEM((TM_FFN, D_FF), BF16),
            pltpu.VMEM((TM_FFN // SUB_FFN, D_MODEL // LANES, SUB_FFN, LANES), F32),
            pltpu.VMEM((TM_FFN // SUB_FFN, D_MODEL // LANES, SUB_FFN, LANES), F32),
        ],
        compiler_params=_params("arbitrary"),
        name="ffn",
    )(x, gpre, wup, cw, cb, wdown, gpost)


def kernel(x, mem, rel_bias, pre_mix_g, w_in, mconv_w, mconv_b, b_igate, b_fgate, attn_out_g,
           mlstm_out_g, w_out, post_mix_g, pre_mem_g, wq_mem, wk_mem, wv_mem, wo_mem, post_mem_g,
           pre_ffn_g, w_up, fconv_w, fconv_b, w_down, post_ffn_g):
    vec3 = lambda t: t.reshape(DEPTH, 1, -1)

    v0 = 3 * ATTN_WIDTH + 2 * MLSTM_WIDTH
    g0 = v0 + 2 * MLSTM_WIDTH
    w_main = w_in[:, :, :v0].astype(BF16)
    w_om = w_in[:, :, v0 + MLSTM_WIDTH:g0].astype(BF16)
    w_vt = jnp.swapaxes(w_in[:, :, v0:v0 + MLSTM_WIDTH], 1, 2).astype(BF16)
    w_gate = jnp.pad(w_in[:, :, g0:], ((0, 0), (0, 0), (0, LANES - GATE_ROWS))).astype(BF16)
    w_gate_t = jnp.pad(jnp.swapaxes(w_in[:, :, g0:], 1, 2), ((0, 0), (0, GATE_ROWS), (0, 0))).astype(BF16)
    gate_bias = jnp.concatenate([b_igate, b_fgate], axis=1)
    gbias = jnp.pad(gate_bias, ((0, 0), (0, LANES - GATE_ROWS)))
    gbias_t = jnp.broadcast_to(gate_bias[:, :, None], (DEPTH, GATE_ROWS, LANES))
    ksc = jnp.concatenate([jnp.ones((1, MLSTM_WIDTH), F32),
                           jnp.full((1, MLSTM_WIDTH), MLSTM_HEAD_DIM ** -0.5, F32)], axis=1)
    w_out_bf = w_out.astype(BF16)
    wq_bf = wq_mem.astype(BF16)
    wo_bf = wo_mem.astype(BF16)
    w_up_bf = w_up.astype(BF16)
    w_down_bf = w_down.astype(BF16)
    bias = _attn_bias_table(rel_bias)

    k_mem, v_mem = _memkv(mem.reshape(BATCH * N_MEM, D_MODEL).astype(BF16), wk_mem, wv_mem)

    xs = x.reshape(TOKENS, D_MODEL)
    for l in range(DEPTH):
        qkv, qk_wide, qk, vt, om, gates, gates_t = _mix_in(
            l, xs, vec3(pre_mix_g), w_main, w_vt, w_om, w_gate, w_gate_t, mconv_w, vec3(mconv_b), ksc)
        attn = _dil_attn(qkv, qk_wide, bias)
        hmn = _mlstm(l, qk, vt, om, gates, gates_t, vec3(gbias), gbias_t, vec3(mlstm_out_g))
        xs = _mix_out(l, xs, attn, hmn, vec3(attn_out_g), w_out_bf, vec3(post_mix_g),
                      vec3(pre_mem_g), wq_bf, k_mem, v_mem, wo_bf, vec3(post_mem_g))
        xs = _ffn(l, xs, vec3(pre_ffn_g), w_up_bf, fconv_w, vec3(fconv_b), w_down_bf,
                  vec3(post_ffn_g))
    return xs.reshape(BATCH, SEQ, D_MODEL)
```

```python
import math

import jax
import jax.numpy as jnp
from jax import lax
from jax.experimental import pallas as pl
from jax.experimental.pallas import tpu as pltpu

F32 = jnp.float32
BF16 = jnp.bfloat16

D_MODEL = 1024
BATCH = 4
SEQ = 4096
DEPTH = 4
TOKENS = BATCH * SEQ

ATTN_HEAD_DIM = 64
ATTN_WIDTH = 512
N_ATTN_HEADS = 8
DILATIONS = (1, 4, 16)
ATTN_BLK = 128
N_MLSTM_HEADS = 4
MLSTM_WIDTH = 512
MLSTM_HEAD_DIM = 128
MLSTM_CHUNK = 128
MLSTM_CONV = 4
N_BUCKETS = 32
MAX_DISTANCE = 2048
N_MEM = 256
N_XHEADS = 4
XHEAD_DIM = 256
D_FF = 2816
FFN_CONV = 3
EPS = 1e-6

LANES = 128
SUBLANES = 8
NEG = -math.inf
VMEM_LIMIT = 56 * 1024 * 1024

TM_IN = 1024
SUB_IN = 256
MIXIN_COLS = 256
TM_OUT = 1024
SUB_OUT = 256
TM_FFN = 512
SUB_FFN = 256
FFN_GROUP = SUBLANES * SUBLANES
FFN_COLS = 256
MAIN_COLS = 3 * ATTN_WIDTH + 2 * MLSTM_WIDTH
GATE_ROWS = 2 * N_MLSTM_HEADS
MLSTM_BATCH = BATCH
MLSTM_OPERAND_ROWS = MLSTM_HEAD_DIM + 16

NT = (((1,), (1,)), ((), ()))


def _dot(a, b):
    return jnp.dot(a, b, preferred_element_type=F32)


def _dot_nt(a, b):
    return lax.dot_general(a, b, NT, preferred_element_type=F32)


def _rms(x, g):
    return x * lax.rsqrt(jnp.mean(x * x, axis=-1, keepdims=True) + EPS) * g


def _sigmoid(x):
    return 1.0 / (1.0 + jnp.exp(-x))


def _zip_phases(gens):
    pending = list(gens)
    active = []
    while pending or active:
        if pending:
            active.append(pending.pop(0))
        for g in list(active):
            try:
                next(g)
            except StopIteration:
                active.remove(g)
        yield


def _interleave(gens):
    for _ in _zip_phases(gens):
        pass


def _resident(block_shape, index_map):
    return pl.BlockSpec(block_shape, index_map, pipeline_mode=pl.Buffered(1))


def _params(*semantics):
    return pltpu.CompilerParams(dimension_semantics=semantics, vmem_limit_bytes=VMEM_LIMIT)


def _memkv_kernel(mem_ref, wk_ref, wv_ref, k_ref, v_ref):
    mem = mem_ref[...]
    k_ref[...] = _dot(mem, wk_ref[...].astype(BF16)).astype(BF16)
    v_ref[...] = _dot(mem, wv_ref[...].astype(BF16)).astype(BF16)


def _memkv(mem_bf, wk, wv):
    rows = BATCH * N_MEM
    w_spec = pl.BlockSpec((None, D_MODEL, D_MODEL), lambda l: (l, 0, 0))
    o_spec = pl.BlockSpec((None, rows, D_MODEL), lambda l: (l, 0, 0))
    o_shape = jax.ShapeDtypeStruct((DEPTH, rows, D_MODEL), BF16)
    return pl.pallas_call(
        _memkv_kernel,
        grid=(DEPTH,),
        in_specs=[_resident((rows, D_MODEL), lambda l: (0, 0)), w_spec, w_spec],
        out_specs=[o_spec, o_spec],
        out_shape=[o_shape, o_shape],
        compiler_params=_params("parallel"),
        name="memkv",
    )(mem_bf, wk, wv)


def _mix_in_kernel(x_ref, g_ref, w_ref, wvt_ref, wom_ref, wg_ref, wgt_ref, cw_ref, cb_ref, ksc_ref,
                   qkv_ref, qkwide_ref, qk_ref, vt_ref, om_ref, gate_ref, gatet_ref, buf_ref):
    @pl.when(pl.program_id(0) % (SEQ // TM_IN) == 0)
    def _():
        buf_ref[0:SUBLANES, :] = jnp.zeros((SUBLANES, 2 * MLSTM_WIDTH), F32)

    a0 = 3 * ATTN_WIDTH
    cw_cols = MIXIN_COLS
    n_conv = 2 * MLSTM_WIDTH // cw_cols

    def sub_tile(r0):
        rows = slice(r0, r0 + SUB_IN)
        h = _rms(x_ref[rows, :], g_ref[...]).astype(BF16)
        yield

        def plain(out_ref, wsrc_ref, c0):
            res = _dot(h, wsrc_ref[:, c0:c0 + cw_cols])
            if out_ref is qkv_ref and c0 < ATTN_WIDTH:
                res = res * (ATTN_HEAD_DIM ** -0.5)
            res = res.astype(BF16)
            out_ref[rows, c0:c0 + cw_cols] = res
            if out_ref is qkv_ref and c0 < 2 * ATTN_WIDTH:
                qkwide_ref[rows, c0:c0 + cw_cols] = res.astype(F32)

        def v_transposed(f0, _):
            vt_ref[f0:f0 + cw_cols, rows] = _dot_nt(wvt_ref[f0:f0 + cw_cols, :], h).astype(BF16)

        plain_jobs = ([(plain, qkv_ref, w_ref, c) for c in range(0, a0, cw_cols)]
                      + [(v_transposed, c, None) for c in range(0, MLSTM_WIDTH, cw_cols)]
                      + [(plain, om_ref, wom_ref, c) for c in range(0, MLSTM_WIDTH, cw_cols)])
        per_conv = -(-len(plain_jobs) // n_conv)
        for cc in range(n_conv):
            cols = slice(cc * cw_cols, (cc + 1) * cw_cols)
            pre = _dot(h, w_ref[:, a0 + cc * cw_cols:a0 + (cc + 1) * cw_cols])
            buf_ref[SUBLANES + r0:SUBLANES + r0 + SUB_IN, cols] = pre
            yield
            for fn, *job in plain_jobs[cc * per_conv:(cc + 1) * per_conv]:
                fn(*job)
            y = cb_ref[:, cols]
            for j in range(MLSTM_CONV - 1):
                lo = SUBLANES + r0 - (MLSTM_CONV - 1) + j
                y = y + buf_ref[lo:lo + SUB_IN, cols] * cw_ref[j:j + 1, cols]
            y = y + pre * cw_ref[MLSTM_CONV - 1:MLSTM_CONV, cols]
            qk_ref[rows, cols] = (y * _sigmoid(y) * ksc_ref[:, cols]).astype(BF16)
            yield
        gate_ref[rows, :] = _dot(h, wg_ref[...])
        gatet_ref[:, rows] = _dot_nt(wgt_ref[...], h)[0:GATE_ROWS, :]

    _interleave([sub_tile(r0) for r0 in range(0, TM_IN, SUB_IN)])
    buf_ref[0:SUBLANES, :] = buf_ref[TM_IN:TM_IN + SUBLANES, :]


def _mix_in(l, x, pre_g, w_main, w_vt, w_om, w_gate, w_gate_t, cw, cb, ksc):
    row = lambda i: (i, 0)
    col = lambda i: (0, i)
    lay = lambda i: (l, 0, 0)
    return pl.pallas_call(
        _mix_in_kernel,
        grid=(TOKENS // TM_IN,),
        in_specs=[
            pl.BlockSpec((TM_IN, D_MODEL), row),
            _resident((None, 1, D_MODEL), lay),
            _resident((None, D_MODEL, MAIN_COLS), lay),
            _resident((None, MLSTM_WIDTH, D_MODEL), lay),
            _resident((None, D_MODEL, MLSTM_WIDTH), lay),
            _resident((None, D_MODEL, LANES), lay),
            _resident((None, 2 * GATE_ROWS, D_MODEL), lay),
            _resident((None, MLSTM_CONV, 2 * MLSTM_WIDTH), lay),
            _resident((None, 1, 2 * MLSTM_WIDTH), lay),
            _resident((1, 2 * MLSTM_WIDTH), lambda i: (0, 0)),
        ],
        out_specs=[
            pl.BlockSpec((TM_IN, 3 * ATTN_WIDTH), row),
            pl.BlockSpec((TM_IN, 2 * ATTN_WIDTH), row),
            pl.BlockSpec((TM_IN, 2 * MLSTM_WIDTH), row),
            pl.BlockSpec((MLSTM_WIDTH, TM_IN), col),
            pl.BlockSpec((TM_IN, MLSTM_WIDTH), row),
            pl.BlockSpec((TM_IN, LANES), row),
            pl.BlockSpec((GATE_ROWS, TM_IN), col),
        ],
        out_shape=[
            jax.ShapeDtypeStruct((TOKENS, 3 * ATTN_WIDTH), BF16),
            jax.ShapeDtypeStruct((TOKENS, 2 * ATTN_WIDTH), F32),
            jax.ShapeDtypeStruct((TOKENS, 2 * MLSTM_WIDTH), BF16),
            jax.ShapeDtypeStruct((MLSTM_WIDTH, TOKENS), BF16),
            jax.ShapeDtypeStruct((TOKENS, MLSTM_WIDTH), BF16),
            jax.ShapeDtypeStruct((TOKENS, LANES), F32),
            jax.ShapeDtypeStruct((GATE_ROWS, TOKENS), F32),
        ],
        scratch_shapes=[pltpu.VMEM((TM_IN + SUBLANES, 2 * MLSTM_WIDTH), F32)],
        compiler_params=_params("arbitrary"),
        name="mix_in",
    )(x, pre_g, w_main, w_vt, w_om, w_gate, w_gate_t, cw, cb, ksc)


N_UNITS = SEQ // ATTN_BLK
PIPE_DEPTH = 4
ONES_ROWS = 16


def _div(u, n):
    return lax.shift_right_logical(u, int(math.log2(n)))


def _mod(u, n):
    return lax.bitwise_and(u, n - 1)


def _dil_attn_kernel(q_ref, k_ref, v_ref, qf, kf, bias_ref, out_ref,
                     vf, q4f, k4f, v4f, dq, dk, dvt, o_s0, o_s1, o_s2, l_s0, l_s1, l_s2):
    d1, d4, d16 = DILATIONS
    hd = ATTN_HEAD_DIM
    dvt[LANES:LANES + ONES_ROWS, :] = jnp.ones((ONES_ROWS, SEQ), BF16)

    def widen(r, carry):
        rows = pl.ds(pl.multiple_of(r * ATTN_BLK, ATTN_BLK), ATTN_BLK)
        v = v_ref[rows, :].astype(F32)
        vf[rows, :] = v
        dvt[0:LANES, rows] = v.T.astype(BF16)
        return carry

    lax.fori_loop(0, N_UNITS, widen, 0, unroll=8)

    lane = lax.broadcasted_iota(jnp.int32, (ATTN_BLK, LANES), 1)
    first_head = lane < hd
    key_in_prev = lax.broadcasted_iota(jnp.int32, (2 * ATTN_BLK, 2 * ATTN_BLK), 0) < ATTN_BLK

    def deinterleave(src_refs, keep_refs, stride, start_of):
        def body(u, carry):
            src = pl.ds(start_of(u), ATTN_BLK, stride=stride)
            dst = pl.ds(pl.multiple_of(u * ATTN_BLK, ATTN_BLK), ATTN_BLK)
            q, k, v = (r[src, :] for r in src_refs)
            if keep_refs is not None:
                for r, val in zip(keep_refs, (q, k, v)):
                    r[dst, :] = val
            dq[dst, :] = q.astype(BF16)
            dk[dst, :] = k.astype(BF16)
            dvt[0:LANES, dst] = v.T.astype(BF16)
            return carry
        lax.fori_loop(0, N_UNITS, body, 0, unroll=8)

    def run_pattern(p, d, q_src, k_src, o_scr, l_scr):
        nb = N_UNITS // d

        def qk(u):
            cur = pl.ds(u * ATTN_BLK, ATTN_BLK)
            prev = pl.ds(max(u - 1, 0) * ATTN_BLK, ATTN_BLK)
            q = q_src[cur, :]
            zero = jnp.zeros_like(q)
            q2 = jnp.concatenate([jnp.where(first_head, q, zero), jnp.where(first_head, zero, q)], axis=0)
            kwin = jnp.concatenate([k_src[prev, :], k_src[cur, :]], axis=0)
            bias = bias_ref[p]
            if u % nb == 0:
                bias = jnp.where(key_in_prev, NEG, bias)
            return _dot_nt(kwin, q2) + bias

        def rest(u, st):
            cur = pl.ds(u * ATTN_BLK, ATTN_BLK)
            prev = pl.ds(max(u - 1, 0) * ATTN_BLK, ATTN_BLK)
            m = jnp.max(st, axis=0, keepdims=True)
            e = jnp.exp(st - m)
            vtwin = jnp.concatenate([dvt[:, prev], dvt[:, cur]], axis=1)
            ot = _dot(vtwin, e.astype(BF16))
            s = ot[LANES:LANES + 1, :]
            inv = 1.0 / s
            lse = m + jnp.log(s)
            o_t = jnp.concatenate([ot[0:hd, 0:ATTN_BLK] * inv[:, 0:ATTN_BLK],
                                   ot[hd:2 * hd, ATTN_BLK:] * inv[:, ATTN_BLK:]], axis=0)
            l_t = jnp.concatenate([jnp.broadcast_to(lse[:, 0:ATTN_BLK], (hd, ATTN_BLK)),
                                   jnp.broadcast_to(lse[:, ATTN_BLK:], (hd, ATTN_BLK))], axis=0)
            dst = pl.ds(u // nb + (u % nb) * (ATTN_BLK * d), ATTN_BLK, stride=d)
            o_scr[dst, :] = o_t.T
            l_scr[dst, :] = l_t.T

        pending = {}
        for u in range(N_UNITS + PIPE_DEPTH):
            if u < N_UNITS:
                pending[u] = qk(u)
            if u >= PIPE_DEPTH:
                rest(u - PIPE_DEPTH, pending.pop(u - PIPE_DEPTH))

    run_pattern(0, d1, q_ref, k_ref, o_s0, l_s0)
    per4 = SEQ // d4
    nb4 = N_UNITS // d4
    deinterleave((qf, kf, vf), (q4f, k4f, v4f), d4,
                 lambda u: _div(u, nb4) + _mod(u, nb4) * (ATTN_BLK * d4))
    run_pattern(1, d4, dq, dk, o_s1, l_s1)
    ratio = d16 // d4
    nb16 = N_UNITS // d16
    deinterleave((q4f, k4f, v4f), None, ratio,
                 lambda u: (_mod(_div(u, nb16), d4) * per4 + _mod(u, nb16) * (ATTN_BLK * ratio)
                            + _div(_div(u, nb16), d4)))
    run_pattern(2, d16, dq, dk, o_s2, l_s2)

    def combine(r, carry):
        rows = pl.ds(pl.multiple_of(r * ATTN_BLK, ATTN_BLK), ATTN_BLK)
        la, lb, lc = l_s0[rows, :], l_s1[rows, :], l_s2[rows, :]
        mx = jnp.maximum(jnp.maximum(la, lb), lc)
        ea, eb, ec = jnp.exp(la - mx), jnp.exp(lb - mx), jnp.exp(lc - mx)
        mixed = (ea * o_s0[rows, :] + eb * o_s1[rows, :] + ec * o_s2[rows, :]) / (ea + eb + ec)
        out_ref[rows, :] = mixed.astype(BF16)
        return carry

    lax.fori_loop(0, N_UNITS, combine, 0, unroll=2)


def _dil_attn(qkv, qk_wide, bias):
    pairs = N_ATTN_HEADS // 2
    blk = (SEQ, LANES)
    f32_seq = pltpu.VMEM((SEQ, LANES), F32)
    bf_seq = pltpu.VMEM((SEQ, LANES), BF16)
    return pl.pallas_call(
        _dil_attn_kernel,
        grid=(BATCH, pairs),
        in_specs=[
            pl.BlockSpec(blk, lambda b, hp: (b, hp)),
            pl.BlockSpec(blk, lambda b, hp: (b, pairs + hp)),
            pl.BlockSpec(blk, lambda b, hp: (b, 2 * pairs + hp)),
            pl.BlockSpec(blk, lambda b, hp: (b, hp)),
            pl.BlockSpec(blk, lambda b, hp: (b, pairs + hp)),
            pl.BlockSpec((None, len(DILATIONS), 2 * ATTN_BLK, 2 * ATTN_BLK),
                         lambda b, hp: (hp, 0, 0, 0)),
        ],
        out_specs=pl.BlockSpec(blk, lambda b, hp: (b, hp)),
        out_shape=jax.ShapeDtypeStruct((TOKENS, ATTN_WIDTH), BF16),
        scratch_shapes=([f32_seq] * 4 + [bf_seq] * 2 + [pltpu.VMEM((LANES + ONES_ROWS, SEQ), BF16)]
                        + [f32_seq] * 6),
        compiler_params=_params("parallel", "parallel"),
        name="dil_attn",
    )(qkv, qkv, qkv, qk_wide, qk_wide, bias)


def _t5_causal_bucket(dist):
    max_exact = N_BUCKETS // 2
    d_f = jnp.maximum(dist, 1).astype(F32)
    large = max_exact + (jnp.log(d_f / max_exact) / math.log(MAX_DISTANCE / max_exact)
                         * (N_BUCKETS - max_exact)).astype(jnp.int32)
    large = jnp.minimum(large, N_BUCKETS - 1)
    return jnp.where(dist < max_exact, dist, large)


def _attn_bias_table(rel_bias):
    pairs = N_ATTN_HEADS // 2
    n_pat = len(DILATIONS)
    t = _attn_bias_qk(rel_bias).reshape(n_pat, pairs, 2, ATTN_BLK, 2 * ATTN_BLK)
    return jnp.transpose(t, (1, 0, 4, 2, 3)).reshape(pairs, n_pat, 2 * ATTN_BLK, 2 * ATTN_BLK)


def _attn_bias_qk(rel_bias):
    blk = ATTN_BLK
    period = 3 * blk + 1
    pad_lo = jnp.full((N_ATTN_HEADS, blk - 1), NEG, F32)
    pad_hi = jnp.full((N_ATTN_HEADS, period - 2 * blk), NEG, F32)
    tables = []
    for d in DILATIONS:
        bias_sub = rel_bias[_t5_causal_bucket(jnp.arange(blk + 1) * d)].T.astype(F32)
        r = jnp.concatenate([pad_lo, bias_sub[:, ::-1], pad_hi], axis=1)
        tiled = jnp.broadcast_to(r[:, None, :], (N_ATTN_HEADS, blk, period))
        skew = tiled.reshape(N_ATTN_HEADS, blk * period)[:, :blk * (period - 1)]
        skew = skew.reshape(N_ATTN_HEADS, blk, period - 1)
        tables.append(skew[:, :, blk - 1:3 * blk - 1])
    return jnp.stack(tables)


def _log_sigmoid(x):
    return jnp.minimum(x, 0.0) - jnp.log1p(jnp.exp(-jnp.abs(x)))


def _split2(x):
    hi = x.astype(BF16)
    return hi, (x - hi.astype(F32)).astype(BF16)


def _split3(x):
    hi, mid = _split2(x)
    lo = (x - hi.astype(F32) - mid.astype(F32)).astype(BF16)
    return hi, mid, lo


def _mlstm_kernel(*refs):
    nb = MLSTM_BATCH
    qk_ref, om_ref, gate_ref = refs[0:3]
    vt_refs = refs[3:3 + nb]
    gatet_refs = refs[3 + nb:3 + 2 * nb]
    gbias_ref, gbiast_ref, gout_ref, out_ref, c_scr, n_scr, m_scr = refs[3 + 2 * nb:]

    @pl.when(pl.program_id(1) == 0)
    def _():
        c_scr[...] = jnp.zeros_like(c_scr)
        n_scr[...] = jnp.zeros_like(n_scr)
        m_scr[...] = jnp.zeros_like(m_scr)

    nh = N_MLSTM_HEADS
    ch = MLSTM_CHUNK
    hd = MLSTM_HEAD_DIM
    row = lax.broadcasted_iota(jnp.int32, (ch, LANES), 0)
    lane = lax.broadcasted_iota(jnp.int32, (ch, LANES), 1)
    src_le_tgt = row <= lane
    tri_lower = (row >= lane).astype(BF16)
    tri_upper = src_le_tgt.astype(BF16)
    grow = lax.broadcasted_iota(jnp.int32, (GATE_ROWS, LANES), 0)
    pad_bf = jnp.zeros((MLSTM_OPERAND_ROWS - hd - 2, ch), BF16)

    def batch_row(bi):
        gates = gate_ref[bi] + gbias_ref[...]
        logf = jnp.where((lane >= nh) & (lane < 2 * nh), _log_sigmoid(gates), 0.0)
        hi, mid, lo = _split3(logf)
        bcum = _dot(tri_lower, hi) + _dot(tri_lower, mid) + _dot(tri_lower, lo)
        gates_t = gatet_refs[bi][...] + gbiast_ref[...]
        logf_t = jnp.where(grow >= nh, _log_sigmoid(gates_t), 0.0)
        hi, mid, lo = _split3(logf_t)
        bcum_t = _dot(hi, tri_upper) + _dot(mid, tri_upper) + _dot(lo, tri_upper)
        yield
        outs = [None] * nh

        def head(h):
            si = bi * nh + h
            hs = slice(h * hd, (h + 1) * hd)
            ks = slice(MLSTM_WIDTH + h * hd, MLSTM_WIDTH + (h + 1) * hd)
            q_h = qk_ref[bi, :, hs]
            k_h = qk_ref[bi, :, ks]
            vt_h = vt_refs[bi][hs, :]
            b_row = bcum_t[nh + h:nh + h + 1, :]
            i_row = gates_t[h:h + 1, :]
            w_col = gates[:, h:h + 1] - bcum[:, nh + h:nh + h + 1]
            g = b_row[:, ch - 1:ch]
            m_prev = m_scr[si:si + 1, :]
            s_t = _dot_nt(k_h, q_h)
            c_prev = c_scr[si]
            n_prev = n_scr[si:si + 1, :]
            carried = _dot_nt(jnp.concatenate([c_prev.astype(BF16), *_split2(n_prev), pad_bf], axis=0), q_h)
            yield
            d_t = jnp.where(src_le_tgt, b_row + w_col, NEG)
            a_row = b_row + m_prev
            m_row = jnp.maximum(a_row, jnp.max(d_t, axis=0, keepdims=True))
            p_t = jnp.exp(d_t - m_row) * s_t
            inter = jnp.exp(a_row - m_row)
            yield
            p_sum = jnp.sum(p_t, axis=0, keepdims=True)
            fresh = _dot(vt_h, p_t.astype(BF16))
            w_row = g - b_row + i_row
            m_new = jnp.maximum(g + m_prev, jnp.max(w_row, axis=1, keepdims=True))
            decay = jnp.exp(g + m_prev - m_new)
            wt = jnp.exp(w_row - m_new)
            upd = _dot(jnp.concatenate([(vt_h.astype(F32) * wt).astype(BF16), *_split2(wt), pad_bf], axis=0), k_h)
            c_scr[si] = decay[:, 0:1] * c_prev + upd[0:hd, :]
            n_scr[si:si + 1, :] = decay[:, 0:1] * n_prev + (upd[hd:hd + 1, :] + upd[hd + 1:hd + 2, :])
            m_scr[si:si + 1, :] = m_new
            yield
            num = inter * carried[0:hd, :] + fresh
            den = inter * (carried[hd:hd + 1, :] + carried[hd + 1:hd + 2, :]) + p_sum
            h_t = num * (1.0 / jnp.maximum(jnp.abs(den), jnp.exp(-m_row)))
            outs[h] = _sigmoid(om_ref[bi, :, hs].astype(F32)) * h_t.T

        for _ in _zip_phases([head(h) for h in range(nh)]):
            yield
        hm = jnp.concatenate(outs, axis=1)
        out_ref[bi] = _rms(hm, gout_ref[...]).astype(BF16)

    for _ in _zip_phases([batch_row(bi) for bi in range(nb)]):
        pass


def _mlstm(l, qk, vt, om, gates, gates_t, gbias, gbias_t, gout):
    nc = SEQ // MLSTM_CHUNK
    nb = MLSTM_BATCH
    blk3 = lambda g, c: (g, c, 0)
    lay = lambda g, c: (l, 0, 0)
    col = lambda bi: (lambda g, c: (0, (g * nb + bi) * nc + c))
    three = lambda t: t.reshape(BATCH, SEQ, t.shape[-1])
    n_chains = nb * N_MLSTM_HEADS
    out = pl.pallas_call(
        _mlstm_kernel,
        grid=(BATCH // nb, nc),
        in_specs=[
            pl.BlockSpec((nb, MLSTM_CHUNK, 2 * MLSTM_WIDTH), blk3),
            pl.BlockSpec((nb, MLSTM_CHUNK, MLSTM_WIDTH), blk3),
            pl.BlockSpec((nb, MLSTM_CHUNK, LANES), blk3),
        ] + [pl.BlockSpec((MLSTM_WIDTH, MLSTM_CHUNK), col(bi)) for bi in range(nb)]
          + [pl.BlockSpec((GATE_ROWS, MLSTM_CHUNK), col(bi)) for bi in range(nb)] + [
            _resident((None, 1, LANES), lay),
            _resident((None, GATE_ROWS, LANES), lay),
            _resident((None, 1, MLSTM_WIDTH), lay),
        ],
        out_specs=pl.BlockSpec((nb, MLSTM_CHUNK, MLSTM_WIDTH), blk3),
        out_shape=jax.ShapeDtypeStruct((BATCH, SEQ, MLSTM_WIDTH), BF16),
        scratch_shapes=[
            pltpu.VMEM((n_chains, MLSTM_HEAD_DIM, MLSTM_HEAD_DIM), F32),
            pltpu.VMEM((n_chains, MLSTM_HEAD_DIM), F32),
            pltpu.VMEM((n_chains, LANES), F32),
        ],
        compiler_params=_params("arbitrary", "arbitrary"),
        name="mlstm",
    )(three(qk), three(om), three(gates), *([vt] * nb), *([gates_t] * nb), gbias, gbias_t, gout)
    return out.reshape(TOKENS, MLSTM_WIDTH)


def _mix_out_kernel(x_ref, attn_ref, hmn_ref, gattn_ref, wout_ref, gpost_ref,
                    gpre_ref, wq_ref, k_ref, v_ref, wo_ref, gpostm_ref, out_ref):
    def sub_tile(r0):
        rows = slice(r0, r0 + SUB_OUT)
        an = _rms(attn_ref[rows, :].astype(F32), gattn_ref[...]).astype(BF16)
        yield
        h = _dot(an, wout_ref[0:ATTN_WIDTH, :]) + _dot(hmn_ref[rows, :], wout_ref[ATTN_WIDTH:, :])
        yield
        x1 = x_ref[rows, :] + _rms(h, gpost_ref[...])
        hq = _rms(x1, gpre_ref[...]).astype(BF16)
        yield
        q = (_dot(hq, wq_ref[...]) * (XHEAD_DIM ** -0.5)).astype(BF16)
        yield
        heads = []
        for hd in range(N_XHEADS):
            sl = slice(hd * XHEAD_DIM, (hd + 1) * XHEAD_DIM)
            logits = _dot_nt(q[:, sl], k_ref[:, sl])
            yield
            m = jnp.max(logits, axis=1, keepdims=True)
            e = jnp.exp(logits - m)
            s = jnp.sum(e, axis=1, keepdims=True)
            yield
            heads.append((_dot(e.astype(BF16), v_ref[:, sl]) / s).astype(BF16))
        o = jnp.concatenate(heads, axis=1)
        yield
        h2 = _dot(o, wo_ref[...])
        yield
        out_ref[rows, :] = x1 + _rms(h2, gpostm_ref[...])

    _interleave([sub_tile(r0) for r0 in range(0, TM_OUT, SUB_OUT)])


def _mix_out(l, x, attn, hmn, gattn, wout, gpost, gpre, wq, k_mem, v_mem, wo, gpostm):
    row = lambda i: (i, 0)
    lay = lambda i: (l, 0, 0)
    tiles_per_seq = SEQ // TM_OUT
    vec = _resident((None, 1, D_MODEL), lay)
    mat = _resident((None, D_MODEL, D_MODEL), lay)
    return pl.pallas_call(
        _mix_out_kernel,
        grid=(TOKENS // TM_OUT,),
        in_specs=[
            pl.BlockSpec((TM_OUT, D_MODEL), row),
            pl.BlockSpec((TM_OUT, ATTN_WIDTH), row),
            pl.BlockSpec((TM_OUT, MLSTM_WIDTH), row),
            _resident((None, 1, ATTN_WIDTH), lay),
            mat, vec, vec, mat,
            pl.BlockSpec((None, N_MEM, D_MODEL), lambda i: (l, i // tiles_per_seq, 0)),
            pl.BlockSpec((None, N_MEM, D_MODEL), lambda i: (l, i // tiles_per_seq, 0)),
            mat, vec,
        ],
        out_specs=pl.BlockSpec((TM_OUT, D_MODEL), row),
        out_shape=jax.ShapeDtypeStruct((TOKENS, D_MODEL), F32),
        compiler_params=_params("parallel"),
        name="mix_out",
    )(x, attn, hmn, gattn, wout, gpost, gpre, wq, k_mem, v_mem, wo, gpostm)


def _ffn_kernel(x_ref, gpre_ref, wup_ref, cw_ref, cb_ref, wdown_ref, gpost_ref, out_ref,
                carry_ref, act_ref, xs_ref, ys_ref):
    assert FFN_CONV == 3

    @pl.when(pl.program_id(0) % (SEQ // TM_FFN) == 0)
    def _():
        carry_ref[...] = jnp.zeros_like(carry_ref)

    n_groups = SUB_FFN // FFN_GROUP
    n_slabs = D_MODEL // LANES
    last_sub = lax.broadcasted_iota(jnp.int32, (SUBLANES, FFN_COLS), 0) == SUBLANES - 1

    for slab, r0 in enumerate(range(0, TM_FFN, SUB_FFN)):
        rows = slice(r0, r0 + SUB_FFN)
        for j in range(n_slabs):
            xs_ref[slab, j] = x_ref[rows, j * LANES:(j + 1) * LANES]
        x = jnp.concatenate([
            jnp.concatenate([xs_ref.at[slab, j][pl.ds(g * FFN_GROUP + v, SUBLANES, stride=SUBLANES), :]
                             for g in range(n_groups) for v in range(SUBLANES)], axis=0)
            for j in range(n_slabs)], axis=1)
        h = _rms(x, gpre_ref[...]).astype(BF16)

        def conv_cols(c0):
            cols = slice(c0, c0 + FFN_COLS)
            u = _dot(h, wup_ref[:, cols])
            vregs = [[u[g * FFN_GROUP + v * SUBLANES:g * FFN_GROUP + (v + 1) * SUBLANES, :]
                      for v in range(SUBLANES)] for g in range(n_groups)]
            prev6 = carry_ref[0:SUBLANES, cols]
            prev7 = carry_ref[SUBLANES:2 * SUBLANES, cols]
            back1, back2 = [], []
            for g in range(n_groups):
                a6 = pltpu.roll(jnp.where(last_sub, prev6, vregs[g][6]), 1, axis=0)
                a7 = pltpu.roll(jnp.where(last_sub, prev7, vregs[g][7]), 1, axis=0)
                back1 += [a7] + vregs[g][0:7]
                back2 += [a6, a7] + vregs[g][0:6]
                prev6, prev7 = vregs[g][6], vregs[g][7]
            carry_ref[0:SUBLANES, cols] = prev6
            carry_ref[SUBLANES:2 * SUBLANES, cols] = prev7
            y = cb_ref[:, cols] + jnp.concatenate(back2, axis=0) * cw_ref[0:1, cols]
            y = y + jnp.concatenate(back1, axis=0) * cw_ref[1:2, cols]
            return y + u * cw_ref[2:3, cols]

        for c in range(D_FF // FFN_COLS):
            a = conv_cols(c * FFN_COLS)
            g = conv_cols(D_FF + c * FFN_COLS)
            gelu = 0.5 * g * (1.0 + jnp.tanh(math.sqrt(2.0 / math.pi) * (g + 0.044715 * (g * g * g))))
            act_ref[rows, c * FFN_COLS:(c + 1) * FFN_COLS] = (gelu * a).astype(BF16)

        out = x + _rms(_dot(act_ref[rows, :], wdown_ref[...]), gpost_ref[...])
        for j in range(n_slabs):
            for g in range(n_groups):
                for v in range(SUBLANES):
                    pr = g * FFN_GROUP + v * SUBLANES
                    dst = pl.ds(g * FFN_GROUP + v, SUBLANES, stride=SUBLANES)
                    ys_ref.at[slab, j][dst, :] = out[pr:pr + SUBLANES, j * LANES:(j + 1) * LANES]
            out_ref[rows, j * LANES:(j + 1) * LANES] = ys_ref[slab, j]


def _ffn(l, x, gpre, wup, cw, cb, wdown, gpost):
    row = lambda i: (i, 0)
    lay = lambda i: (l, 0, 0)
    vec = _resident((None, 1, D_MODEL), lay)
    return pl.pallas_call(
        _ffn_kernel,
        grid=(TOKENS // TM_FFN,),
        in_specs=[
            pl.BlockSpec((TM_FFN, D_MODEL), row),
            vec,
            _resident((None, D_MODEL, 2 * D_FF), lay),
            _resident((None, FFN_CONV, 2 * D_FF), lay),
            _resident((None, 1, 2 * D_FF), lay),
            _resident((None, D_FF, D_MODEL), lay),
            vec,
        ],
        out_specs=pl.BlockSpec((TM_FFN, D_MODEL), row),
        out_shape=jax.ShapeDtypeStruct((TOKENS, D_MODEL), F32),
        scratch_shapes=[
            pltpu.VMEM((2 * SUBLANES, 2 * D_FF), F32),
            pltpu.VMEM((TM_FFN, D_FF), BF16),
            pltpu.VMEM((TM_FFN // SUB_FFN, D_MODEL // LANES, SUB_FFN, LANES), F32),
            pltpu.VMEM((TM_FFN // SUB_FFN, D_MODEL // LANES, SUB_FFN, LANES), F32),
        ],
        compiler_params=_params("arbitrary"),
        name="ffn",
    )(x, gpre, wup, cw, cb, wdown, gpost)


def kernel(x, mem, rel_bias, pre_mix_g, w_in, mconv_w, mconv_b, b_igate, b_fgate, attn_out_g,
           mlstm_out_g, w_out, post_mix_g, pre_mem_g, wq_mem, wk_mem, wv_mem, wo_mem, post_mem_g,
           pre_ffn_g, w_up, fconv_w, fconv_b, w_down, post_ffn_g):
    vec3 = lambda t: t.reshape(DEPTH, 1, -1)

    v0 = 3 * ATTN_WIDTH + 2 * MLSTM_WIDTH
    g0 = v0 + 2 * MLSTM_WIDTH
    w_main = w_in[:, :, :v0].astype(BF16)
    w_om = w_in[:, :, v0 + MLSTM_WIDTH:g0].astype(BF16)
    w_vt = jnp.swapaxes(w_in[:, :, v0:v0 + MLSTM_WIDTH], 1, 2).astype(BF16)
    w_gate = jnp.pad(w_in[:, :, g0:], ((0, 0), (0, 0), (0, LANES - GATE_ROWS))).astype(BF16)
    w_gate_t = jnp.pad(jnp.swapaxes(w_in[:, :, g0:], 1, 2), ((0, 0), (0, GATE_ROWS), (0, 0))).astype(BF16)
    gate_bias = jnp.concatenate([b_igate, b_fgate], axis=1)
    gbias = jnp.pad(gate_bias, ((0, 0), (0, LANES - GATE_ROWS)))
    gbias_t = jnp.broadcast_to(gate_bias[:, :, None], (DEPTH, GATE_ROWS, LANES))
    ksc = jnp.concatenate([jnp.ones((1, MLSTM_WIDTH), F32),
                           jnp.full((1, MLSTM_WIDTH), MLSTM_HEAD_DIM ** -0.5, F32)], axis=1)
    w_out_bf = w_out.astype(BF16)
    wq_bf = wq_mem.astype(BF16)
    wo_bf = wo_mem.astype(BF16)
    w_up_bf = w_up.astype(BF16)
    w_down_bf = w_down.astype(BF16)
    bias = _attn_bias_table(rel_bias)

    k_mem, v_mem = _memkv(mem.reshape(BATCH * N_MEM, D_MODEL).astype(BF16), wk_mem, wv_mem)

    xs = x.reshape(TOKENS, D_MODEL)
    for l in range(DEPTH):
        qkv, qk_wide, qk, vt, om, gates, gates_t = _mix_in(
            l, xs, vec3(pre_mix_g), w_main, w_vt, w_om, w_gate, w_gate_t, mconv_w, vec3(mconv_b), ksc)
        attn = _dil_attn(qkv, qk_wide, bias)
        hmn = _mlstm(l, qk, vt, om, gates, gates_t, vec3(gbias), gbias_t, vec3(mlstm_out_g))
        xs = _mix_out(l, xs, attn, hmn, vec3(attn_out_g), w_out_bf, vec3(post_mix_g),
                      vec3(pre_mem_g), wq_bf, k_mem, v_mem, wo_bf, vec3(post_mem_g))
        xs = _ffn(l, xs, vec3(pre_ffn_g), w_up_bf, fconv_w, vec3(fconv_b), w_down_bf,
                  vec3(post_ffn_g))
    return xs.reshape(BATCH, SEQ, D_MODEL)
```

```python
import math

import jax
import jax.numpy as jnp
from jax import lax
from jax.experimental import pallas as pl
from jax.experimental.pallas import tpu as pltpu

F32 = jnp.float32
BF16 = jnp.bfloat16

D_MODEL = 1024
BATCH = 4
SEQ = 4096
DEPTH = 4
TOKENS = BATCH * SEQ

ATTN_HEAD_DIM = 64
ATTN_WIDTH = 512
N_ATTN_HEADS = 8
DILATIONS = (1, 4, 16)
ATTN_BLK = 128
N_MLSTM_HEADS = 4
MLSTM_WIDTH = 512
MLSTM_HEAD_DIM = 128
MLSTM_CHUNK = 128
MLSTM_CONV = 4
N_BUCKETS = 32
MAX_DISTANCE = 2048
N_MEM = 256
N_XHEADS = 4
XHEAD_DIM = 256
D_FF = 2816
FFN_CONV = 3
EPS = 1e-6

LANES = 128
SUBLANES = 8
NEG = -math.inf
VMEM_LIMIT = 56 * 1024 * 1024

TM_IN = 1024
SUB_IN = 256
MIXIN_COLS = 256
TM_OUT = 1024
SUB_OUT = 256
TM_FFN = 512
SUB_FFN = 256
FFN_GROUP = SUBLANES * SUBLANES
FFN_COLS = 256
MAIN_COLS = 3 * ATTN_WIDTH + 2 * MLSTM_WIDTH
GATE_ROWS = 2 * N_MLSTM_HEADS
MLSTM_BATCH = BATCH
MLSTM_OPERAND_ROWS = MLSTM_HEAD_DIM + 16

NT = (((1,), (1,)), ((), ()))


def _dot(a, b):
    return jnp.dot(a, b, preferred_element_type=F32)


def _dot_nt(a, b):
    return lax.dot_general(a, b, NT, preferred_element_type=F32)


def _rms(x, g):
    return x * lax.rsqrt(jnp.mean(x * x, axis=-1, keepdims=True) + EPS) * g


def _sigmoid(x):
    return 1.0 / (1.0 + jnp.exp(-x))


def _zip_phases(gens):
    pending = list(gens)
    active = []
    while pending or active:
        if pending:
            active.append(pending.pop(0))
        for g in list(active):
            try:
                next(g)
            except StopIteration:
                active.remove(g)
        yield


def _interleave(gens):
    for _ in _zip_phases(gens):
        pass


def _resident(block_shape, index_map):
    return pl.BlockSpec(block_shape, index_map, pipeline_mode=pl.Buffered(1))


def _params(*semantics, fuse_inputs=None):
    return pltpu.CompilerParams(dimension_semantics=semantics, vmem_limit_bytes=VMEM_LIMIT,
                                allow_input_fusion=fuse_inputs)


def _memkv_kernel(mem_ref, wk_ref, wv_ref, k_ref, v_ref):
    mem = mem_ref[...]
    k_ref[...] = _dot(mem, wk_ref[...].astype(BF16)).astype(BF16)
    v_ref[...] = _dot(mem, wv_ref[...].astype(BF16)).astype(BF16)


def _memkv(mem_bf, wk, wv):
    rows = BATCH * N_MEM
    w_spec = pl.BlockSpec((None, D_MODEL, D_MODEL), lambda l: (l, 0, 0))
    o_spec = pl.BlockSpec((None, rows, D_MODEL), lambda l: (l, 0, 0))
    o_shape = jax.ShapeDtypeStruct((DEPTH, rows, D_MODEL), BF16)
    return pl.pallas_call(
        _memkv_kernel,
        grid=(DEPTH,),
        in_specs=[_resident((rows, D_MODEL), lambda l: (0, 0)), w_spec, w_spec],
        out_specs=[o_spec, o_spec],
        out_shape=[o_shape, o_shape],
        compiler_params=_params("parallel"),
        name="memkv",
    )(mem_bf, wk, wv)


def _mix_in_kernel(x_ref, g_ref, w_ref, wvt_ref, wom_ref, wg_ref, wgt_ref, cw_ref, cb_ref, ksc_ref,
                   qkv_ref, qkwide_ref, qk_ref, vt_ref, om_ref, gate_ref, gatet_ref, buf_ref):
    @pl.when(pl.program_id(0) % (SEQ // TM_IN) == 0)
    def _():
        buf_ref[0:SUBLANES, :] = jnp.zeros((SUBLANES, 2 * MLSTM_WIDTH), F32)

    a0 = 3 * ATTN_WIDTH
    cw_cols = MIXIN_COLS
    n_conv = 2 * MLSTM_WIDTH // cw_cols

    def sub_tile(r0):
        rows = slice(r0, r0 + SUB_IN)
        h = _rms(x_ref[rows, :], g_ref[...]).astype(BF16)
        yield

        def plain(out_ref, wsrc_ref, c0):
            res = _dot(h, wsrc_ref[:, c0:c0 + cw_cols])
            if out_ref is qkv_ref and c0 < ATTN_WIDTH:
                res = res * (ATTN_HEAD_DIM ** -0.5)
            res = res.astype(BF16)
            out_ref[rows, c0:c0 + cw_cols] = res
            if out_ref is qkv_ref and c0 < 2 * ATTN_WIDTH:
                qkwide_ref[rows, c0:c0 + cw_cols] = res.astype(F32)

        def v_transposed(f0, _):
            vt_ref[f0:f0 + cw_cols, rows] = _dot_nt(wvt_ref[f0:f0 + cw_cols, :], h).astype(BF16)

        plain_jobs = ([(plain, qkv_ref, w_ref, c) for c in range(0, a0, cw_cols)]
                      + [(v_transposed, c, None) for c in range(0, MLSTM_WIDTH, cw_cols)]
                      + [(plain, om_ref, wom_ref, c) for c in range(0, MLSTM_WIDTH, cw_cols)])
        per_conv = -(-len(plain_jobs) // n_conv)
        for cc in range(n_conv):
            cols = slice(cc * cw_cols, (cc + 1) * cw_cols)
            pre = _dot(h, w_ref[:, a0 + cc * cw_cols:a0 + (cc + 1) * cw_cols])
            buf_ref[SUBLANES + r0:SUBLANES + r0 + SUB_IN, cols] = pre
            yield
            for fn, *job in plain_jobs[cc * per_conv:(cc + 1) * per_conv]:
                fn(*job)
            y = cb_ref[:, cols]
            for j in range(MLSTM_CONV - 1):
                lo = SUBLANES + r0 - (MLSTM_CONV - 1) + j
                y = y + buf_ref[lo:lo + SUB_IN, cols] * cw_ref[j:j + 1, cols]
            y = y + pre * cw_ref[MLSTM_CONV - 1:MLSTM_CONV, cols]
            qk_ref[rows, cols] = (y * _sigmoid(y) * ksc_ref[:, cols]).astype(BF16)
            yield
        gate_ref[rows, :] = _dot(h, wg_ref[...])
        gatet_ref[:, rows] = _dot_nt(wgt_ref[...], h)[0:GATE_ROWS, :]

    _interleave([sub_tile(r0) for r0 in range(0, TM_IN, SUB_IN)])
    buf_ref[0:SUBLANES, :] = buf_ref[TM_IN:TM_IN + SUBLANES, :]


def _mix_in(l, x, pre_g, w_main, w_vt, w_om, w_gate, w_gate_t, cw, cb, ksc):
    row = lambda i: (i, 0)
    col = lambda i: (0, i)
    lay = lambda i: (l, 0, 0)
    return pl.pallas_call(
        _mix_in_kernel,
        grid=(TOKENS // TM_IN,),
        in_specs=[
            pl.BlockSpec((TM_IN, D_MODEL), row),
            _resident((None, 1, D_MODEL), lay),
            _resident((None, D_MODEL, MAIN_COLS), lay),
            _resident((None, MLSTM_WIDTH, D_MODEL), lay),
            _resident((None, D_MODEL, MLSTM_WIDTH), lay),
            _resident((None, D_MODEL, LANES), lay),
            _resident((None, 2 * GATE_ROWS, D_MODEL), lay),
            _resident((None, MLSTM_CONV, 2 * MLSTM_WIDTH), lay),
            _resident((None, 1, 2 * MLSTM_WIDTH), lay),
            _resident((1, 2 * MLSTM_WIDTH), lambda i: (0, 0)),
        ],
        out_specs=[
            pl.BlockSpec((TM_IN, 3 * ATTN_WIDTH), row),
            pl.BlockSpec((TM_IN, 2 * ATTN_WIDTH), row),
            pl.BlockSpec((TM_IN, 2 * MLSTM_WIDTH), row),
            pl.BlockSpec((MLSTM_WIDTH, TM_IN), col),
            pl.BlockSpec((TM_IN, MLSTM_WIDTH), row),
            pl.BlockSpec((TM_IN, LANES), row),
            pl.BlockSpec((GATE_ROWS, TM_IN), col),
        ],
        out_shape=[
            jax.ShapeDtypeStruct((TOKENS, 3 * ATTN_WIDTH), BF16),
            jax.ShapeDtypeStruct((TOKENS, 2 * ATTN_WIDTH), F32),
            jax.ShapeDtypeStruct((TOKENS, 2 * MLSTM_WIDTH), BF16),
            jax.ShapeDtypeStruct((MLSTM_WIDTH, TOKENS), BF16),
            jax.ShapeDtypeStruct((TOKENS, MLSTM_WIDTH), BF16),
            jax.ShapeDtypeStruct((TOKENS, LANES), F32),
            jax.ShapeDtypeStruct((GATE_ROWS, TOKENS), F32),
        ],
        scratch_shapes=[pltpu.VMEM((TM_IN + SUBLANES, 2 * MLSTM_WIDTH), F32)],
        compiler_params=_params("arbitrary"),
        name="mix_in",
    )(x, pre_g, w_main, w_vt, w_om, w_gate, w_gate_t, cw, cb, ksc)


N_UNITS = SEQ // ATTN_BLK
PIPE_DEPTH = 4
ONES_ROWS = 16


def _div(u, n):
    return lax.shift_right_logical(u, int(math.log2(n)))


def _mod(u, n):
    return lax.bitwise_and(u, n - 1)


def _dil_attn_kernel(q_ref, k_ref, v_ref, qf, kf, bias_ref, out_ref,
                     vf, q4f, k4f, v4f, dq, dk, dvt, o_s0, o_s1, o_s2, l_s0, l_s1, l_s2):
    d1, d4, d16 = DILATIONS
    hd = ATTN_HEAD_DIM
    dvt[LANES:LANES + ONES_ROWS, :] = jnp.ones((ONES_ROWS, SEQ), BF16)

    def widen(r, carry):
        rows = pl.ds(pl.multiple_of(r * ATTN_BLK, ATTN_BLK), ATTN_BLK)
        v = v_ref[rows, :].astype(F32)
        vf[rows, :] = v
        dvt[0:LANES, rows] = v.T.astype(BF16)
        return carry

    lax.fori_loop(0, N_UNITS, widen, 0, unroll=8)

    lane = lax.broadcasted_iota(jnp.int32, (ATTN_BLK, LANES), 1)
    first_head = lane < hd
    key_in_prev = lax.broadcasted_iota(jnp.int32, (2 * ATTN_BLK, 2 * ATTN_BLK), 0) < ATTN_BLK

    def deinterleave(src_refs, keep_refs, stride, start_of):
        def body(u, carry):
            src = pl.ds(start_of(u), ATTN_BLK, stride=stride)
            dst = pl.ds(pl.multiple_of(u * ATTN_BLK, ATTN_BLK), ATTN_BLK)
            q, k, v = (r[src, :] for r in src_refs)
            if keep_refs is not None:
                for r, val in zip(keep_refs, (q, k, v)):
                    r[dst, :] = val
            dq[dst, :] = q.astype(BF16)
            dk[dst, :] = k.astype(BF16)
            dvt[0:LANES, dst] = v.T.astype(BF16)
            return carry
        lax.fori_loop(0, N_UNITS, body, 0, unroll=8)

    def run_pattern(p, d, q_src, k_src, o_scr, l_scr):
        nb = N_UNITS // d

        def qk(u):
            cur = pl.ds(u * ATTN_BLK, ATTN_BLK)
            prev = pl.ds(max(u - 1, 0) * ATTN_BLK, ATTN_BLK)
            q = q_src[cur, :]
            zero = jnp.zeros_like(q)
            q2 = jnp.concatenate([jnp.where(first_head, q, zero), jnp.where(first_head, zero, q)], axis=0)
            kwin = jnp.concatenate([k_src[prev, :], k_src[cur, :]], axis=0)
            bias = bias_ref[p]
            if u % nb == 0:
                bias = jnp.where(key_in_prev, NEG, bias)
            return _dot_nt(kwin, q2) + bias

        def rest(u, st):
            cur = pl.ds(u * ATTN_BLK, ATTN_BLK)
            prev = pl.ds(max(u - 1, 0) * ATTN_BLK, ATTN_BLK)
            m = jnp.max(st, axis=0, keepdims=True)
            e = jnp.exp(st - m)
            vtwin = jnp.concatenate([dvt[:, prev], dvt[:, cur]], axis=1)
            ot = _dot(vtwin, e.astype(BF16))
            s = ot[LANES:LANES + 1, :]
            inv = 1.0 / s
            lse = m + jnp.log(s)
            o_t = jnp.concatenate([ot[0:hd, 0:ATTN_BLK] * inv[:, 0:ATTN_BLK],
                                   ot[hd:2 * hd, ATTN_BLK:] * inv[:, ATTN_BLK:]], axis=0)
            l_t = jnp.concatenate([jnp.broadcast_to(lse[:, 0:ATTN_BLK], (hd, ATTN_BLK)),
                                   jnp.broadcast_to(lse[:, ATTN_BLK:], (hd, ATTN_BLK))], axis=0)
            dst = pl.ds(u // nb + (u % nb) * (ATTN_BLK * d), ATTN_BLK, stride=d)
            o_scr[dst, :] = o_t.T
            l_scr[dst, :] = l_t.T

        pending = {}
        for u in range(N_UNITS + PIPE_DEPTH):
            if u < N_UNITS:
                pending[u] = qk(u)
            if u >= PIPE_DEPTH:
                rest(u - PIPE_DEPTH, pending.pop(u - PIPE_DEPTH))

    run_pattern(0, d1, q_ref, k_ref, o_s0, l_s0)
    per4 = SEQ // d4
    nb4 = N_UNITS // d4
    deinterleave((qf, kf, vf), (q4f, k4f, v4f), d4,
                 lambda u: _div(u, nb4) + _mod(u, nb4) * (ATTN_BLK * d4))
    run_pattern(1, d4, dq, dk, o_s1, l_s1)
    ratio = d16 // d4
    nb16 = N_UNITS // d16
    deinterleave((q4f, k4f, v4f), None, ratio,
                 lambda u: (_mod(_div(u, nb16), d4) * per4 + _mod(u, nb16) * (ATTN_BLK * ratio)
                            + _div(_div(u, nb16), d4)))
    run_pattern(2, d16, dq, dk, o_s2, l_s2)

    def combine(r, carry):
        rows = pl.ds(pl.multiple_of(r * ATTN_BLK, ATTN_BLK), ATTN_BLK)
        la, lb, lc = l_s0[rows, :], l_s1[rows, :], l_s2[rows, :]
        mx = jnp.maximum(jnp.maximum(la, lb), lc)
        ea, eb, ec = jnp.exp(la - mx), jnp.exp(lb - mx), jnp.exp(lc - mx)
        mixed = (ea * o_s0[rows, :] + eb * o_s1[rows, :] + ec * o_s2[rows, :]) / (ea + eb + ec)
        out_ref[rows, :] = mixed.astype(BF16)
        return carry

    lax.fori_loop(0, N_UNITS, combine, 0, unroll=2)


def _dil_attn(qkv, qk_wide, bias):
    pairs = N_ATTN_HEADS // 2
    blk = (SEQ, LANES)
    f32_seq = pltpu.VMEM((SEQ, LANES), F32)
    bf_seq = pltpu.VMEM((SEQ, LANES), BF16)
    return pl.pallas_call(
        _dil_attn_kernel,
        grid=(BATCH, pairs),
        in_specs=[
            pl.BlockSpec(blk, lambda b, hp: (b, hp)),
            pl.BlockSpec(blk, lambda b, hp: (b, pairs + hp)),
            pl.BlockSpec(blk, lambda b, hp: (b, 2 * pairs + hp)),
            pl.BlockSpec(blk, lambda b, hp: (b, hp)),
            pl.BlockSpec(blk, lambda b, hp: (b, pairs + hp)),
            pl.BlockSpec((None, len(DILATIONS), 2 * ATTN_BLK, 2 * ATTN_BLK),
                         lambda b, hp: (hp, 0, 0, 0)),
        ],
        out_specs=pl.BlockSpec(blk, lambda b, hp: (b, hp)),
        out_shape=jax.ShapeDtypeStruct((TOKENS, ATTN_WIDTH), BF16),
        scratch_shapes=([f32_seq] * 4 + [bf_seq] * 2 + [pltpu.VMEM((LANES + ONES_ROWS, SEQ), BF16)]
                        + [f32_seq] * 6),
        compiler_params=_params("parallel", "parallel"),
        name="dil_attn",
    )(qkv, qkv, qkv, qk_wide, qk_wide, bias)


def _t5_causal_bucket(dist):
    max_exact = N_BUCKETS // 2
    d_f = jnp.maximum(dist, 1).astype(F32)
    large = max_exact + (jnp.log(d_f / max_exact) / math.log(MAX_DISTANCE / max_exact)
                         * (N_BUCKETS - max_exact)).astype(jnp.int32)
    large = jnp.minimum(large, N_BUCKETS - 1)
    return jnp.where(dist < max_exact, dist, large)


def _attn_bias_table(rel_bias):
    pairs = N_ATTN_HEADS // 2
    n_pat = len(DILATIONS)
    t = _attn_bias_qk(rel_bias).reshape(n_pat, pairs, 2, ATTN_BLK, 2 * ATTN_BLK)
    return jnp.transpose(t, (1, 0, 4, 2, 3)).reshape(pairs, n_pat, 2 * ATTN_BLK, 2 * ATTN_BLK)


def _attn_bias_qk(rel_bias):
    blk = ATTN_BLK
    period = 3 * blk + 1
    pad_lo = jnp.full((N_ATTN_HEADS, blk - 1), NEG, F32)
    pad_hi = jnp.full((N_ATTN_HEADS, period - 2 * blk), NEG, F32)
    tables = []
    for d in DILATIONS:
        bias_sub = rel_bias[_t5_causal_bucket(jnp.arange(blk + 1) * d)].T.astype(F32)
        r = jnp.concatenate([pad_lo, bias_sub[:, ::-1], pad_hi], axis=1)
        tiled = jnp.broadcast_to(r[:, None, :], (N_ATTN_HEADS, blk, period))
        skew = tiled.reshape(N_ATTN_HEADS, blk * period)[:, :blk * (period - 1)]
        skew = skew.reshape(N_ATTN_HEADS, blk, period - 1)
        tables.append(skew[:, :, blk - 1:3 * blk - 1])
    return jnp.stack(tables)


def _log_sigmoid(x):
    return jnp.minimum(x, 0.0) - jnp.log1p(jnp.exp(-jnp.abs(x)))


def _split2(x):
    hi = x.astype(BF16)
    return hi, (x - hi.astype(F32)).astype(BF16)


def _split3(x):
    hi, mid = _split2(x)
    lo = (x - hi.astype(F32) - mid.astype(F32)).astype(BF16)
    return hi, mid, lo


def _mlstm_kernel(*refs):
    nb = MLSTM_BATCH
    qk_ref, om_ref, gate_ref = refs[0:3]
    vt_refs = refs[3:3 + nb]
    gatet_refs = refs[3 + nb:3 + 2 * nb]
    gbias_ref, gbiast_ref, gout_ref, out_ref, c_scr, n_scr, m_scr = refs[3 + 2 * nb:]

    @pl.when(pl.program_id(1) == 0)
    def _():
        c_scr[...] = jnp.zeros_like(c_scr)
        n_scr[...] = jnp.zeros_like(n_scr)
        m_scr[...] = jnp.zeros_like(m_scr)

    nh = N_MLSTM_HEADS
    ch = MLSTM_CHUNK
    hd = MLSTM_HEAD_DIM
    row = lax.broadcasted_iota(jnp.int32, (ch, LANES), 0)
    lane = lax.broadcasted_iota(jnp.int32, (ch, LANES), 1)
    src_le_tgt = row <= lane
    tri_lower = (row >= lane).astype(BF16)
    tri_upper = src_le_tgt.astype(BF16)
    grow = lax.broadcasted_iota(jnp.int32, (GATE_ROWS, LANES), 0)
    pad_bf = jnp.zeros((MLSTM_OPERAND_ROWS - hd - 2, ch), BF16)

    def batch_row(bi):
        gates = gate_ref[bi] + gbias_ref[...]
        logf = jnp.where((lane >= nh) & (lane < 2 * nh), _log_sigmoid(gates), 0.0)
        hi, mid, lo = _split3(logf)
        bcum = _dot(tri_lower, hi) + _dot(tri_lower, mid) + _dot(tri_lower, lo)
        gates_t = gatet_refs[bi][...] + gbiast_ref[...]
        logf_t = jnp.where(grow >= nh, _log_sigmoid(gates_t), 0.0)
        hi, mid, lo = _split3(logf_t)
        bcum_t = _dot(hi, tri_upper) + _dot(mid, tri_upper) + _dot(lo, tri_upper)
        yield
        outs = [None] * nh

        def head(h):
            si = bi * nh + h
            hs = slice(h * hd, (h + 1) * hd)
            ks = slice(MLSTM_WIDTH + h * hd, MLSTM_WIDTH + (h + 1) * hd)
            q_h = qk_ref[bi, :, hs]
            k_h = qk_ref[bi, :, ks]
            vt_h = vt_refs[bi][hs, :]
            b_row = bcum_t[nh + h:nh + h + 1, :]
            i_row = gates_t[h:h + 1, :]
            w_col = gates[:, h:h + 1] - bcum[:, nh + h:nh + h + 1]
            g = b_row[:, ch - 1:ch]
            m_prev = m_scr[si:si + 1, :]
            s_t = _dot_nt(k_h, q_h)
            c_prev = c_scr[si]
            n_prev = n_scr[si:si + 1, :]
            carried = _dot_nt(jnp.concatenate([c_prev.astype(BF16), *_split2(n_prev), pad_bf], axis=0), q_h)
            yield
            d_t = jnp.where(src_le_tgt, b_row + w_col, NEG)
            a_row = b_row + m_prev
            m_row = jnp.maximum(a_row, jnp.max(d_t, axis=0, keepdims=True))
            p_t = jnp.exp(d_t - m_row) * s_t
            inter = jnp.exp(a_row - m_row)
            yield
            p_sum = jnp.sum(p_t, axis=0, keepdims=True)
            fresh = _dot(vt_h, p_t.astype(BF16))
            w_row = g - b_row + i_row
            m_new = jnp.maximum(g + m_prev, jnp.max(w_row, axis=1, keepdims=True))
            decay = jnp.exp(g + m_prev - m_new)
            wt = jnp.exp(w_row - m_new)
            upd = _dot(jnp.concatenate([(vt_h.astype(F32) * wt).astype(BF16), *_split2(wt), pad_bf], axis=0), k_h)
            c_scr[si] = decay[:, 0:1] * c_prev + upd[0:hd, :]
            n_scr[si:si + 1, :] = decay[:, 0:1] * n_prev + (upd[hd:hd + 1, :] + upd[hd + 1:hd + 2, :])
            m_scr[si:si + 1, :] = m_new
            yield
            num = inter * carried[0:hd, :] + fresh
            den = inter * (carried[hd:hd + 1, :] + carried[hd + 1:hd + 2, :]) + p_sum
            h_t = num * (1.0 / jnp.maximum(jnp.abs(den), jnp.exp(-m_row)))
            outs[h] = _sigmoid(om_ref[bi, :, hs].astype(F32)) * h_t.T

        for _ in _zip_phases([head(h) for h in range(nh)]):
            yield
        hm = jnp.concatenate(outs, axis=1)
        out_ref[bi] = _rms(hm, gout_ref[...]).astype(BF16)

    for _ in _zip_phases([batch_row(bi) for bi in range(nb)]):
        pass


def _mlstm(l, qk, vt, om, gates, gates_t, gbias, gbias_t, gout):
    nc = SEQ // MLSTM_CHUNK
    nb = MLSTM_BATCH
    blk3 = lambda g, c: (g, c, 0)
    lay = lambda g, c: (l, 0, 0)
    col = lambda bi: (lambda g, c: (0, (g * nb + bi) * nc + c))
    three = lambda t: t.reshape(BATCH, SEQ, t.shape[-1])
    n_chains = nb * N_MLSTM_HEADS
    out = pl.pallas_call(
        _mlstm_kernel,
        grid=(BATCH // nb, nc),
        in_specs=[
            pl.BlockSpec((nb, MLSTM_CHUNK, 2 * MLSTM_WIDTH), blk3),
            pl.BlockSpec((nb, MLSTM_CHUNK, MLSTM_WIDTH), blk3),
            pl.BlockSpec((nb, MLSTM_CHUNK, LANES), blk3),
        ] + [pl.BlockSpec((MLSTM_WIDTH, MLSTM_CHUNK), col(bi)) for bi in range(nb)]
          + [pl.BlockSpec((GATE_ROWS, MLSTM_CHUNK), col(bi)) for bi in range(nb)] + [
            _resident((None, 1, LANES), lay),
            _resident((None, GATE_ROWS, LANES), lay),
            _resident((None, 1, MLSTM_WIDTH), lay),
        ],
        out_specs=pl.BlockSpec((nb, MLSTM_CHUNK, MLSTM_WIDTH), blk3),
        out_shape=jax.ShapeDtypeStruct((BATCH, SEQ, MLSTM_WIDTH), BF16),
        scratch_shapes=[
            pltpu.VMEM((n_chains, MLSTM_HEAD_DIM, MLSTM_HEAD_DIM), F32),
            pltpu.VMEM((n_chains, MLSTM_HEAD_DIM), F32),
            pltpu.VMEM((n_chains, LANES), F32),
        ],
        compiler_params=_params("arbitrary", "arbitrary"),
        name="mlstm",
    )(three(qk), three(om), three(gates), *([vt] * nb), *([gates_t] * nb), gbias, gbias_t, gout)
    return out.reshape(TOKENS, MLSTM_WIDTH)


def _mix_out_kernel(x_ref, attn_ref, hmn_ref, gattn_ref, wout_ref, gpost_ref,
                    gpre_ref, wq_ref, k_ref, v_ref, wo_ref, gpostm_ref, out_ref):
    def sub_tile(r0):
        rows = slice(r0, r0 + SUB_OUT)
        an = _rms(attn_ref[rows, :].astype(F32), gattn_ref[...]).astype(BF16)
        yield
        h = _dot(an, wout_ref[0:ATTN_WIDTH, :]) + _dot(hmn_ref[rows, :], wout_ref[ATTN_WIDTH:, :])
        yield
        x1 = x_ref[rows, :] + _rms(h, gpost_ref[...])
        hq = _rms(x1, gpre_ref[...]).astype(BF16)
        yield
        q = (_dot(hq, wq_ref[...]) * (XHEAD_DIM ** -0.5)).astype(BF16)
        yield
        heads = []
        for hd in range(N_XHEADS):
            sl = slice(hd * XHEAD_DIM, (hd + 1) * XHEAD_DIM)
            logits = _dot_nt(q[:, sl], k_ref[:, sl])
            yield
            m = jnp.max(logits, axis=1, keepdims=True)
            e = jnp.exp(logits - m)
            s = jnp.sum(e, axis=1, keepdims=True)
            yield
            heads.append((_dot(e.astype(BF16), v_ref[:, sl]) / s).astype(BF16))
        o = jnp.concatenate(heads, axis=1)
        yield
        h2 = _dot(o, wo_ref[...])
        yield
        out_ref[rows, :] = x1 + _rms(h2, gpostm_ref[...])

    _interleave([sub_tile(r0) for r0 in range(0, TM_OUT, SUB_OUT)])


def _mix_out(l, x, attn, hmn, gattn, wout, gpost, gpre, wq, k_mem, v_mem, wo, gpostm):
    row = lambda i: (i, 0)
    lay = lambda i: (l, 0, 0)
    tiles_per_seq = SEQ // TM_OUT
    vec = _resident((None, 1, D_MODEL), lay)
    mat = _resident((None, D_MODEL, D_MODEL), lay)
    return pl.pallas_call(
        _mix_out_kernel,
        grid=(TOKENS // TM_OUT,),
        in_specs=[
            pl.BlockSpec((TM_OUT, D_MODEL), row),
            pl.BlockSpec((TM_OUT, ATTN_WIDTH), row),
            pl.BlockSpec((TM_OUT, MLSTM_WIDTH), row),
            _resident((None, 1, ATTN_WIDTH), lay),
            mat, vec, vec, mat,
            pl.BlockSpec((None, N_MEM, D_MODEL), lambda i: (l, i // tiles_per_seq, 0)),
            pl.BlockSpec((None, N_MEM, D_MODEL), lambda i: (l, i // tiles_per_seq, 0)),
            mat, vec,
        ],
        out_specs=pl.BlockSpec((TM_OUT, D_MODEL), row),
        out_shape=jax.ShapeDtypeStruct((TOKENS, D_MODEL), F32),
        compiler_params=_params("parallel"),
        name="mix_out",
    )(x, attn, hmn, gattn, wout, gpost, gpre, wq, k_mem, v_mem, wo, gpostm)


def _ffn_kernel(x_ref, gpre_ref, wup_ref, cw_ref, cb_ref, wdown_ref, gpost_ref, out_ref,
                carry_ref, act_ref, xs_ref, ys_ref):
    assert FFN_CONV == 3

    @pl.when(pl.program_id(0) % (SEQ // TM_FFN) == 0)
    def _():
        carry_ref[...] = jnp.zeros_like(carry_ref)

    n_groups = SUB_FFN // FFN_GROUP
    n_slabs = D_MODEL // LANES
    last_sub = lax.broadcasted_iota(jnp.int32, (SUBLANES, FFN_COLS), 0) == SUBLANES - 1

    for slab, r0 in enumerate(range(0, TM_FFN, SUB_FFN)):
        rows = slice(r0, r0 + SUB_FFN)
        for j in range(n_slabs):
            xs_ref[slab, j] = x_ref[rows, j * LANES:(j + 1) * LANES]
        x = jnp.concatenate([
            jnp.concatenate([xs_ref.at[slab, j][pl.ds(g * FFN_GROUP + v, SUBLANES, stride=SUBLANES), :]
                             for g in range(n_groups) for v in range(SUBLANES)], axis=0)
            for j in range(n_slabs)], axis=1)
        h = _rms(x, gpre_ref[...]).astype(BF16)

        def conv_cols(c0):
            cols = slice(c0, c0 + FFN_COLS)
            u = _dot(h, wup_ref[:, cols])
            vregs = [[u[g * FFN_GROUP + v * SUBLANES:g * FFN_GROUP + (v + 1) * SUBLANES, :]
                      for v in range(SUBLANES)] for g in range(n_groups)]
            prev6 = carry_ref[0:SUBLANES, cols]
            prev7 = carry_ref[SUBLANES:2 * SUBLANES, cols]
            back1, back2 = [], []
            for g in range(n_groups):
                a6 = pltpu.roll(jnp.where(last_sub, prev6, vregs[g][6]), 1, axis=0)
                a7 = pltpu.roll(jnp.where(last_sub, prev7, vregs[g][7]), 1, axis=0)
                back1 += [a7] + vregs[g][0:7]
                back2 += [a6, a7] + vregs[g][0:6]
                prev6, prev7 = vregs[g][6], vregs[g][7]
            carry_ref[0:SUBLANES, cols] = prev6
            carry_ref[SUBLANES:2 * SUBLANES, cols] = prev7
            y = cb_ref[:, cols] + jnp.concatenate(back2, axis=0) * cw_ref[0:1, cols]
            y = y + jnp.concatenate(back1, axis=0) * cw_ref[1:2, cols]
            return y + u * cw_ref[2:3, cols]

        for c in range(D_FF // FFN_COLS):
            a = conv_cols(c * FFN_COLS)
            g = conv_cols(D_FF + c * FFN_COLS)
            gelu = 0.5 * g * (1.0 + jnp.tanh(math.sqrt(2.0 / math.pi) * (g + 0.044715 * (g * g * g))))
            act_ref[rows, c * FFN_COLS:(c + 1) * FFN_COLS] = (gelu * a).astype(BF16)

        out = x + _rms(_dot(act_ref[rows, :], wdown_ref[...]), gpost_ref[...])
        for j in range(n_slabs):
            for g in range(n_groups):
                for v in range(SUBLANES):
                    pr = g * FFN_GROUP + v * SUBLANES
                    dst = pl.ds(g * FFN_GROUP + v, SUBLANES, stride=SUBLANES)
                    ys_ref.at[slab, j][dst, :] = out[pr:pr + SUBLANES, j * LANES:(j + 1) * LANES]
            out_ref[rows, j * LANES:(j + 1) * LANES] = ys_ref[slab, j]


def _ffn(l, x, gpre, wup, cw, cb, wdown, gpost):
    row = lambda i: (i, 0)
    lay = lambda i: (l, 0, 0)
    vec = _resident((None, 1, D_MODEL), lay)
    return pl.pallas_call(
        _ffn_kernel,
        grid=(TOKENS // TM_FFN,),
        in_specs=[
            pl.BlockSpec((TM_FFN, D_MODEL), row),
            vec,
            _resident((D_MODEL, 2 * D_FF), lambda i: (0, 0)),
            _resident((None, FFN_CONV, 2 * D_FF), lay),
            _resident((None, 1, 2 * D_FF), lay),
            _resident((D_FF, D_MODEL), lambda i: (0, 0)),
            vec,
        ],
        out_specs=pl.BlockSpec((TM_FFN, D_MODEL), row),
        out_shape=jax.ShapeDtypeStruct((TOKENS, D_MODEL), F32),
        scratch_shapes=[
            pltpu.VMEM((2 * SUBLANES, 2 * D_FF), F32),
            pltpu.VMEM((TM_FFN, D_FF), BF16),
            pltpu.VMEM((TM_FFN // SUB_FFN, D_MODEL // LANES, SUB_FFN, LANES), F32),
            pltpu.VMEM((TM_FFN // SUB_FFN, D_MODEL // LANES, SUB_FFN, LANES), F32),
        ],
        compiler_params=_params("arbitrary",
                                fuse_inputs=[False, False, True, False, False, True, False]),
        name="ffn",
    )(x, gpre, wup, cw, cb, wdown, gpost)


def kernel(x, mem, rel_bias, pre_mix_g, w_in, mconv_w, mconv_b, b_igate, b_fgate, attn_out_g,
           mlstm_out_g, w_out, post_mix_g, pre_mem_g, wq_mem, wk_mem, wv_mem, wo_mem, post_mem_g,
           pre_ffn_g, w_up, fconv_w, fconv_b, w_down, post_ffn_g):
    vec3 = lambda t: t.reshape(DEPTH, 1, -1)

    v0 = 3 * ATTN_WIDTH + 2 * MLSTM_WIDTH
    g0 = v0 + 2 * MLSTM_WIDTH
    w_main = w_in[:, :, :v0].astype(BF16)
    w_om = w_in[:, :, v0 + MLSTM_WIDTH:g0].astype(BF16)
    w_vt = jnp.swapaxes(w_in[:, :, v0:v0 + MLSTM_WIDTH], 1, 2).astype(BF16)
    w_gate = jnp.pad(w_in[:, :, g0:], ((0, 0), (0, 0), (0, LANES - GATE_ROWS))).astype(BF16)
    w_gate_t = jnp.pad(jnp.swapaxes(w_in[:, :, g0:], 1, 2), ((0, 0), (0, GATE_ROWS), (0, 0))).astype(BF16)
    gate_bias = jnp.concatenate([b_igate, b_fgate], axis=1)
    gbias = jnp.pad(gate_bias, ((0, 0), (0, LANES - GATE_ROWS)))
    gbias_t = jnp.broadcast_to(gate_bias[:, :, None], (DEPTH, GATE_ROWS, LANES))
    ksc = jnp.concatenate([jnp.ones((1, MLSTM_WIDTH), F32),
                           jnp.full((1, MLSTM_WIDTH), MLSTM_HEAD_DIM ** -0.5, F32)], axis=1)
    w_out_bf = w_out.astype(BF16)
    wq_bf = wq_mem.astype(BF16)
    wo_bf = wo_mem.astype(BF16)
    bias = _attn_bias_table(rel_bias)

    k_mem, v_mem = _memkv(mem.reshape(BATCH * N_MEM, D_MODEL).astype(BF16), wk_mem, wv_mem)

    xs = x.reshape(TOKENS, D_MODEL)
    for l in range(DEPTH):
        qkv, qk_wide, qk, vt, om, gates, gates_t = _mix_in(
            l, xs, vec3(pre_mix_g), w_main, w_vt, w_om, w_gate, w_gate_t, mconv_w, vec3(mconv_b), ksc)
        attn = _dil_attn(qkv, qk_wide, bias)
        hmn = _mlstm(l, qk, vt, om, gates, gates_t, vec3(gbias), gbias_t, vec3(mlstm_out_g))
        xs = _mix_out(l, xs, attn, hmn, vec3(attn_out_g), w_out_bf, vec3(post_mix_g),
                      vec3(pre_mem_g), wq_bf, k_mem, v_mem, wo_bf, vec3(post_mem_g))
        xs = _ffn(l, xs, vec3(pre_ffn_g), w_up[l].astype(BF16), fconv_w, vec3(fconv_b),
                  w_down[l].astype(BF16),
                  vec3(post_ffn_g))
    return xs.reshape(BATCH, SEQ, D_MODEL)
```
